```python
import jax, jax.numpy as jnp
from jax import lax
import numpy as np

D_MODEL = 1024
BATCH = 8
SEQ = 8192
DEPTH = 2
DEC_BATCH = 1
DEC_SEQ = 16384
PAST_LEN = 128

GRID_W = 64
NA_HEADS = D_MODEL // 128
NA_HEAD_DIM = 64
NA_WIDTH = NA_HEADS * NA_HEAD_DIM
NA_KH = 8
NA_KW = 16
SC_WIDTH = D_MODEL // 2
SC_KERNEL = 3
CF_WIDTH = D_MODEL // 2
CF_KERNEL = 31
N_BRANCHES = 3
IN_COLS = 3 * NA_WIDTH + 3 * SC_WIDTH + 2 * CF_WIDTH + N_BRANCHES * D_MODEL
MOE_GROUPS = 4
MOE_EXPERTS_PER_GROUP = 4
MOE_EXPERTS = MOE_GROUPS * MOE_EXPERTS_PER_GROUP
MOE_TOP_K = 2
D_EXPERT = D_MODEL // 4
EPS = 1e-6

kernel_name = "hybrid_na_shortconv_conformer_hmoe_encoder"


def rmsnorm(x, g):
    xf = x.astype(jnp.float32)
    y = xf * lax.rsqrt(jnp.mean(xf * xf, axis=-1, keepdims=True) + EPS)
    return (y * g.astype(jnp.float32)).astype(x.dtype)


def layernorm(x, g, b):
    xf = x.astype(jnp.float32)
    mu = jnp.mean(xf, axis=-1, keepdims=True)
    xc = xf - mu
    y = xc * lax.rsqrt(jnp.mean(xc * xc, axis=-1, keepdims=True) + EPS)
    return (y * g.astype(jnp.float32) + b.astype(jnp.float32)).astype(x.dtype)


def depthwise_conv(u, w):
    k = w.shape[0]
    c = u.shape[-1]
    return lax.conv_general_dilated(
        u, w[:, None, :].astype(u.dtype), window_strides=(1,),
        padding=[(k // 2, k // 2)], dimension_numbers=('NWC', 'WIO', 'NWC'),
        feature_group_count=c)


def neighbourhood_attention(q, k, v, rpb):
    bn, seq_len = q.shape[0], q.shape[1]
    rows = seq_len // GRID_W
    kh = min(NA_KH, rows)
    kw = NA_KW
    grid = (bn, rows, GRID_W, NA_HEADS, NA_HEAD_DIM)
    q, k, v = q.reshape(grid), k.reshape(grid), v.reshape(grid)
    row_start = jnp.clip(jnp.arange(rows) - kh // 2, 0, rows - kh)
    cols = jnp.arange(GRID_W)
    col_start = jnp.clip(cols - kw // 2, 0, GRID_W - kw)
    col_keys = col_start[:, None] + jnp.arange(kw)[None, :]
    col_off = col_keys - cols[:, None] + (NA_KW - 1)
    rpb_cols = rpb.astype(jnp.float32)[:, :, col_off]
    scale = NA_HEAD_DIM ** -0.5

    def one_row(r):
        rs = row_start[r]
        q_r = lax.dynamic_index_in_dim(q, r, axis=1, keepdims=False)
        k_rows = lax.dynamic_slice_in_dim(k, rs, kh, axis=1)
        v_rows = lax.dynamic_slice_in_dim(v, rs, kh, axis=1)
        k_win = k_rows[:, :, col_keys]
        v_win = v_rows[:, :, col_keys]
        row_off = rs + jnp.arange(kh) - r + (NA_KH - 1)
        bias = jnp.transpose(rpb_cols[:, row_off], (0, 2, 1, 3))
        s = jnp.einsum('bchd,bicjhd->bhcij', q_r, k_win).astype(jnp.float32) * scale + bias[None]
        p = jax.nn.softmax(s.reshape(bn, NA_HEADS, GRID_W, kh * kw), axis=-1)
        p = p.reshape(bn, NA_HEADS, GRID_W, kh, kw).astype(v.dtype)
        return jnp.einsum('bhcij,bicjhd->bchd', p, v_win)

    out = lax.map(one_row, jnp.arange(rows))
    return jnp.moveaxis(out, 0, 1).reshape(bn, seq_len, NA_WIDTH)


def mixer_block(h, w_in, b_in, q_g, k_g, rpb, sc_w, cf_w, cf_b, cf_g, cf_beta, w_branch, w_out):
    bn, seq_len, _ = h.shape
    z = jnp.einsum('bld,dc->blc', h, w_in) + b_in
    sizes = [NA_WIDTH] * 3 + [SC_WIDTH] * 3 + [CF_WIDTH] * 2
    splits = [int(s) for s in np.cumsum(sizes)]
    q, k, v, sc_x, sc_b, sc_c, cf_a, cf_gate, gate_logits = jnp.split(z, splits, axis=-1)
    hs = (bn, seq_len, NA_HEADS, NA_HEAD_DIM)
    q = rmsnorm(q.reshape(hs), q_g)
    k = rmsnorm(k.reshape(hs), k_g)
    y_a = neighbourhood_attention(q, k, v.reshape(hs), rpb)
    y_b = sc_b * depthwise_conv(sc_c * sc_x, sc_w)
    u = cf_a * jax.nn.sigmoid(cf_gate)
    u = depthwise_conv(u, cf_w) + cf_b
    y_c = jax.nn.silu(layernorm(u, cf_g, cf_beta))
    gates = jax.nn.sigmoid(gate_logits.reshape(bn, seq_len, N_BRANCHES, D_MODEL))
    merged = (gates[:, :, 0] * jnp.einsum('blc,cd->bld', y_a, w_branch[0])
              + gates[:, :, 1] * jnp.einsum('blc,cd->bld', y_b, w_branch[1])
              + gates[:, :, 2] * jnp.einsum('blc,cd->bld', y_c, w_branch[2]))
    return jnp.einsum('bld,de->ble', merged, w_out)


def hierarchical_moe(h, wg, bg, we, be, w_gate, w_up, w_down):
    bn, seq_len, _ = h.shape
    hf = h.reshape(bn * seq_len, D_MODEL)
    n_tok = hf.shape[0]
    g_logits = (hf @ wg + bg).astype(jnp.float32)
    g_val, g_idx = lax.top_k(jax.nn.softmax(g_logits, axis=-1), 1)
    e_logits = (hf @ we + be).astype(jnp.float32).reshape(n_tok, MOE_GROUPS, MOE_EXPERTS_PER_GROUP)
    e_sel = jnp.take_along_axis(e_logits, g_idx[:, :, None], axis=1)[:, 0]
    e_val, e_idx = lax.top_k(e_sel, MOE_TOP_K)
    weights = jax.nn.softmax(e_val, axis=-1) * g_val
    expert_id = g_idx * MOE_EXPERTS_PER_GROUP + e_idx
    comb = jnp.sum(jax.nn.one_hot(expert_id, MOE_EXPERTS, dtype=jnp.float32) * weights[..., None], axis=1)
    comb = comb.astype(h.dtype)
    y = jnp.zeros_like(hf)
    for e in range(MOE_EXPERTS):
        act = jax.nn.silu(hf @ w_gate[e]) * (hf @ w_up[e])
        y = y + comb[:, e:e + 1] * (act @ w_down[e])
    return y.reshape(bn, seq_len, D_MODEL)


def trunk(x, norm_mix_g, w_in, b_in, q_norm_g, k_norm_g, na_rpb, sc_conv_w, cf_conv_w, cf_conv_b,
          cf_norm_g, cf_norm_b, w_branch, w_out, norm_ffn_g, router_group_w, router_group_b,
          router_expert_w, router_expert_b, expert_w_gate, expert_w_up, expert_w_down):
    for l in range(DEPTH):
        h = rmsnorm(x, norm_mix_g[l])
        x = x + mixer_block(h, w_in[l], b_in[l], q_norm_g[l], k_norm_g[l], na_rpb[l], sc_conv_w[l],
                            cf_conv_w[l], cf_conv_b[l], cf_norm_g[l], cf_norm_b[l], w_branch[l], w_out[l])
        h = rmsnorm(x, norm_ffn_g[l])
        x = x + hierarchical_moe(h, router_group_w[l], router_group_b[l], router_expert_w[l],
                                 router_expert_b[l], expert_w_gate[l], expert_w_up[l], expert_w_down[l])
    return x


def setup_inputs(seed: int = 0) -> dict:
    key = jax.random.key(seed)
    ks = jax.random.split(key, 23)

    def nrm(k, shape, scale):
        return jax.random.normal(k, shape, jnp.float32) * scale

    L = DEPTH
    return {
        "x_prompt": nrm(ks[0], (BATCH, SEQ, D_MODEL), 1.0),
        "x_sample": nrm(ks[1], (DEC_BATCH, DEC_SEQ, D_MODEL), 1.0),
        "norm_mix_g": 1.0 + nrm(ks[2], (L, D_MODEL), 0.02),
        "w_in": nrm(ks[3], (L, D_MODEL, IN_COLS), D_MODEL ** -0.5),
        "b_in": nrm(ks[4], (L, IN_COLS), 0.02),
        "q_norm_g": 1.0 + nrm(ks[5], (L, NA_HEAD_DIM), 0.02),
        "k_norm_g": 1.0 + nrm(ks[6], (L, NA_HEAD_DIM), 0.02),
        "na_rpb": nrm(ks[7], (L, NA_HEADS, 2 * NA_KH - 1, 2 * NA_KW - 1), 0.1),
        "sc_conv_w": nrm(ks[8], (L, SC_KERNEL, SC_WIDTH), SC_KERNEL ** -0.5),
        "cf_conv_w": nrm(ks[9], (L, CF_KERNEL, CF_WIDTH), CF_KERNEL ** -0.5),
        "cf_conv_b": nrm(ks[10], (L, CF_WIDTH), 0.02),
        "cf_norm_g": 1.0 + nrm(ks[11], (L, CF_WIDTH), 0.02),
        "cf_norm_b": nrm(ks[12], (L, CF_WIDTH), 0.02),
        "w_branch": nrm(ks[13], (L, N_BRANCHES, NA_WIDTH, D_MODEL), NA_WIDTH ** -0.5),
        "w_out": nrm(ks[14], (L, D_MODEL, D_MODEL), D_MODEL ** -0.5),
        "norm_ffn_g": 1.0 + nrm(ks[15], (L, D_MODEL), 0.02),
        "router_group_w": nrm(ks[16], (L, D_MODEL, MOE_GROUPS), D_MODEL ** -0.5),
        "router_group_b": nrm(ks[17], (L, MOE_GROUPS), 0.01),
        "router_expert_w": nrm(ks[18], (L, D_MODEL, MOE_EXPERTS), D_MODEL ** -0.5),
        "router_expert_b": nrm(ks[19], (L, MOE_EXPERTS), 0.01),
        "expert_w_gate": nrm(ks[20], (L, MOE_EXPERTS, D_MODEL, D_EXPERT), D_MODEL ** -0.5),
        "expert_w_up": nrm(ks[21], (L, MOE_EXPERTS, D_MODEL, D_EXPERT), D_MODEL ** -0.5),
        "expert_w_down": nrm(ks[22], (L, MOE_EXPERTS, D_EXPERT, D_MODEL), D_EXPERT ** -0.5),
    }


def reference(x_prompt, x_sample, norm_mix_g, w_in, b_in, q_norm_g, k_norm_g, na_rpb, sc_conv_w,
              cf_conv_w, cf_conv_b, cf_norm_g, cf_norm_b, w_branch, w_out, norm_ffn_g,
              router_group_w, router_group_b, router_expert_w, router_expert_b,
              expert_w_gate, expert_w_up, expert_w_down):
    params = (norm_mix_g, w_in, b_in, q_norm_g, k_norm_g, na_rpb, sc_conv_w, cf_conv_w, cf_conv_b,
              cf_norm_g, cf_norm_b, w_branch, w_out, norm_ffn_g, router_group_w, router_group_b,
              router_expert_w, router_expert_b, expert_w_gate, expert_w_up, expert_w_down)
    y_prompt = trunk(x_prompt, *params)
    y_sample = trunk(x_sample, *params)
    return (y_prompt, y_sample)
```

```python
import functools

import numpy as np
import jax
import jax.numpy as jnp
from jax import lax
from jax.experimental import pallas as pl
from jax.experimental.pallas import tpu as pltpu

F32 = jnp.float32
BF16 = jnp.bfloat16

D_MODEL = 1024
GRID_W = 64
NA_HEADS = 8
NA_HEAD_DIM = 64
NA_WIDTH = NA_HEADS * NA_HEAD_DIM
NA_KH = 8
NA_KW = 16
SC_WIDTH = 512
SC_KERNEL = 3
CF_WIDTH = 512
CF_KERNEL = 31
IN_COLS = 3 * NA_WIDTH + 3 * SC_WIDTH + 2 * CF_WIDTH + 3 * D_MODEL
MOE_GROUPS = 4
MOE_EPG = 4
MOE_EXPERTS = MOE_GROUPS * MOE_EPG
D_EXPERT = 256
EPS = 1e-6

N_PAIRS = MOE_EPG * (MOE_EPG - 1) // 2
N_CLASSES = MOE_GROUPS * N_PAIRS
CLS_ROWS = 32
ROW_W = D_MODEL + 128
MASK_VALUE = -1e30

TM = 512
RB = 8
HALO = 16
CONV_CHUNK = 64
TME = 256
VMEM_LIMIT = 56 * 1024 * 1024


def _cparams(*sem):
    return pltpu.CompilerParams(dimension_semantics=sem, vmem_limit_bytes=VMEM_LIMIT)


def _resident(shape):
    nd = len(shape)
    return pl.BlockSpec(shape, lambda *_: (0,) * nd, pipeline_mode=pl.Buffered(1))


def _inproj_kernel(x_ref, g_ref, w_ref, b_ref, qg_ref, kg_ref, bd_ref, qkv_ref, scu_ref, gates_ref):
    x = x_ref[...]
    ms = jnp.mean(x * x, axis=-1, keepdims=True)
    h = (x * lax.rsqrt(ms + EPS) * g_ref[...]).astype(BF16)

    def zcols(c0, c1):
        return jnp.dot(h, w_ref[:, c0:c1], preferred_element_type=F32) + b_ref[:, c0:c1]

    def head_norm(z, gain_ref):
        hm = jnp.dot((z * z).astype(BF16), bd_ref[...], preferred_element_type=F32)
        return z * lax.rsqrt(hm + EPS) * gain_ref[...]

    w = NA_WIDTH
    qkv_ref[:, 0:w] = head_norm(zcols(0, w), qg_ref).astype(BF16)
    qkv_ref[:, w:2 * w] = head_norm(zcols(w, 2 * w), kg_ref).astype(BF16)
    qkv_ref[:, 2 * w:3 * w] = zcols(2 * w, 3 * w).astype(BF16)
    o = 3 * w
    sc_x = zcols(o, o + 512)
    sc_b = zcols(o + 512, o + 1024)
    sc_c = zcols(o + 1024, o + 1536)
    scu_ref[:, 0:512] = (sc_c * sc_x).astype(BF16)
    scu_ref[:, 512:1024] = sc_b.astype(BF16)
    cf_a = zcols(o + 1536, o + 2048)
    cf_gate = zcols(o + 2048, o + 2560)
    scu_ref[:, 1024:1536] = (cf_a * jax.nn.sigmoid(cf_gate)).astype(BF16)
    o = o + 2560
    for c in range(0, 3 * D_MODEL, 512):
        gates_ref[:, c:c + 512] = jax.nn.sigmoid(zcols(o + c, o + c + 512)).astype(BF16)


def _inproj(x, g, w, b, qg, kg, bd):
    n = x.shape[0]
    return pl.pallas_call(
        _inproj_kernel,
        grid=(n // TM,),
        in_specs=[
            pl.BlockSpec((TM, D_MODEL), lambda i: (i, 0)),
            _resident((1, D_MODEL)),
            _resident((D_MODEL, IN_COLS)),
            _resident((1, IN_COLS)),
            _resident((1, NA_WIDTH)),
            _resident((1, NA_WIDTH)),
            _resident((NA_WIDTH, NA_WIDTH)),
        ],
        out_specs=[
            pl.BlockSpec((TM, 3 * NA_WIDTH), lambda i: (i, 0)),
            pl.BlockSpec((TM, 1536), lambda i: (i, 0)),
            pl.BlockSpec((TM, 3 * D_MODEL), lambda i: (i, 0)),
        ],
        out_shape=[
            jax.ShapeDtypeStruct((n, 3 * NA_WIDTH), BF16),
            jax.ShapeDtypeStruct((n, 1536), BF16),
            jax.ShapeDtypeStruct((n, 3 * D_MODEL), BF16),
        ],
        compiler_params=_cparams("parallel"),
        name="inproj",
    )(x, g, w, b, qg, kg, bd)


def _attn_kernel(rows, q_ref, kp_ref, kc_ref, kn_ref, vp_ref, vc_ref, vn_ref, t_ref, o_ref, kbuf, vbuf):
    blk = RB * GRID_W
    kbuf[0:blk] = kp_ref[...]
    kbuf[blk:2 * blk] = kc_ref[...]
    kbuf[2 * blk:3 * blk] = kn_ref[...]
    vbuf[0:blk] = vp_ref[...]
    vbuf[blk:2 * blk] = vc_ref[...]
    vbuf[2 * blk:3 * blk] = vn_ref[...]
    r0 = pl.program_id(1) * RB
    lane = lax.broadcasted_iota(jnp.int32, (GRID_W, 128), 1)
    low = lane < NA_HEAD_DIM
    nkeys = NA_KH * GRID_W

    def row_body(i, carry):
        r = r0 + i
        rs = jnp.clip(r - NA_KH // 2, 0, rows - NA_KH)
        d = r - rs
        start = pl.multiple_of((rs - r0 + RB) * GRID_W, GRID_W)
        qrow = pl.multiple_of(i * GRID_W, GRID_W)
        for j in range(NA_HEADS // 2):
            cols = slice(128 * j, 128 * (j + 1))
            qp = q_ref[pl.ds(qrow, GRID_W), cols]
            kw = kbuf[pl.ds(start, nkeys), cols]
            vw = vbuf[pl.ds(start, nkeys), cols]
            outs = []
            for hh in range(2):
                qm = jnp.where(low if hh == 0 else jnp.logical_not(low), qp, jnp.zeros_like(qp))
                s = lax.dot_general(qm, kw, (((1,), (1,)), ((), ())), preferred_element_type=F32)
                s = s + t_ref[2 * j + hh, d]
                m = jnp.max(s, axis=-1, keepdims=True)
                p = jnp.exp(s - m)
                l = jnp.sum(p, axis=-1, keepdims=True)
                o = jnp.dot(p.astype(BF16), vw, preferred_element_type=F32)
                outs.append(o / l)
            o_ref[pl.ds(qrow, GRID_W), cols] = jnp.where(low, outs[0], outs[1]).astype(BF16)
        return carry

    lax.fori_loop(0, RB, row_body, 0)


def _attention(qkv, bn, seq_len, tbias):
    rows = seq_len // GRID_W
    assert rows >= NA_KH and rows % RB == 0 and RB >= NA_KH // 2
    nrb = rows // RB
    blk = RB * GRID_W
    n = bn * seq_len

    def at(col, shift):
        def imap(b, i):
            return (b * nrb + jnp.clip(i + shift, 0, nrb - 1), col)
        return pl.BlockSpec((blk, NA_WIDTH), imap)

    return pl.pallas_call(
        functools.partial(_attn_kernel, rows),
        grid=(bn, nrb),
        in_specs=[at(0, 0), at(1, -1), at(1, 0), at(1, 1), at(2, -1), at(2, 0), at(2, 1),
                  _resident(tbias.shape)],
        out_specs=pl.BlockSpec((blk, NA_WIDTH), lambda b, i: (b * nrb + i, 0)),
        out_shape=jax.ShapeDtypeStruct((n, NA_WIDTH), BF16),
        scratch_shapes=[pltpu.VMEM((3 * blk, NA_WIDTH), BF16), pltpu.VMEM((3 * blk, NA_WIDTH), BF16)],
        compiler_params=_cparams("parallel", "parallel"),
        name="nattn",
    )(qkv, qkv, qkv, qkv, qkv, qkv, qkv, tbias)


def _bias_table(rpb):
    cols = np.arange(GRID_W)
    cs = np.clip(cols - NA_KW // 2, 0, GRID_W - NA_KW)
    kc = np.arange(GRID_W)
    inside = (kc[None, :] >= cs[:, None]) & (kc[None, :] < cs[:, None] + NA_KW)
    co = np.clip(kc[None, :] - cols[:, None] + NA_KW - 1, 0, 2 * NA_KW - 2)
    d = np.arange(NA_KH)
    i = np.arange(NA_KH)
    ro = i[None, :] - d[:, None] + NA_KH - 1
    t = rpb.astype(F32)[:, ro[:, :, None, None], co[None, None, :, :]]
    t = jnp.where(inside[None, None, None], t, MASK_VALUE)
    t = jnp.transpose(t, (0, 1, 3, 2, 4))
    return t.reshape(NA_HEADS, NA_KH, GRID_W, NA_KH * GRID_W)


def _merge_kernel(x_ref, ya_ref, scu_ref, hp_ref, hn_ref, gates_ref, scw_ref, cfw_ref, cfb_ref, cfg_ref,
                  cfbeta_ref, wb_ref, wo_ref, o_ref, ext_s, ext_u, yb_buf, yc_buf):
    t = pl.program_id(1)
    nt = pl.num_programs(1)
    keep_p = (t > 0).astype(F32)
    keep_n = (t < nt - 1).astype(F32)
    ext_s[0:HALO] = hp_ref[:, 0:512].astype(F32) * keep_p
    ext_s[HALO:HALO + TM] = scu_ref[:, 0:512].astype(F32)
    ext_s[HALO + TM:] = hn_ref[:, 0:512].astype(F32) * keep_n
    ext_u[0:HALO] = hp_ref[:, 1024:1536].astype(F32) * keep_p
    ext_u[HALO:HALO + TM] = scu_ref[:, 1024:1536].astype(F32)
    ext_u[HALO + TM:] = hn_ref[:, 1024:1536].astype(F32) * keep_n

    for ci in range(TM // CONV_CHUNK):
        base = ci * CONV_CHUNK
        acc = jnp.zeros((CONV_CHUNK, CF_WIDTH), F32)
        for k in range(CF_KERNEL):
            off = HALO - CF_KERNEL // 2 + k
            acc = acc + ext_u[pl.ds(base + off, CONV_CHUNK), :] * cfw_ref[k:k + 1, :]
        acc = acc + cfb_ref[...]
        mu = jnp.mean(acc, axis=-1, keepdims=True)
        xc = acc - mu
        var = jnp.mean(xc * xc, axis=-1, keepdims=True)
        y = xc * lax.rsqrt(var + EPS) * cfg_ref[...] + cfbeta_ref[...]
        yc_buf[pl.ds(base, CONV_CHUNK), :] = (y * jax.nn.sigmoid(y)).astype(BF16)
        accb = jnp.zeros((CONV_CHUNK, SC_WIDTH), F32)
        for k in range(SC_KERNEL):
            off = HALO - SC_KERNEL // 2 + k
            accb = accb + ext_s[pl.ds(base + off, CONV_CHUNK), :] * scw_ref[k:k + 1, :]
        scb = scu_ref[pl.ds(base, CONV_CHUNK), 512:1024].astype(F32)
        yb_buf[pl.ds(base, CONV_CHUNK), :] = (scb * accb).astype(BF16)

    merged = gates_ref[:, 0:D_MODEL].astype(F32) * jnp.dot(ya_ref[...], wb_ref[0], preferred_element_type=F32)
    merged = merged + gates_ref[:, D_MODEL:2 * D_MODEL].astype(F32) * jnp.dot(
        yb_buf[...], wb_ref[1], preferred_element_type=F32)
    merged = merged + gates_ref[:, 2 * D_MODEL:].astype(F32) * jnp.dot(
        yc_buf[...], wb_ref[2], preferred_element_type=F32)
    o_ref[...] = x_ref[...] + jnp.dot(merged.astype(BF16), wo_ref[...], preferred_element_type=F32)


def _merge(x, ya, scu, gates, bn, seq_len, scw, cfw, cfb, cfg, cfbeta, wb, wo):
    n = bn * seq_len
    nt = seq_len // TM
    hb = TM // HALO

    def tile(width):
        return pl.BlockSpec((TM, width), lambda b, t: (b * nt + t, 0))

    halo_prev = pl.BlockSpec((HALO, 1536), lambda b, t: (jnp.maximum((b * nt + t) * hb - 1, 0), 0))
    halo_next = pl.BlockSpec((HALO, 1536), lambda b, t: (jnp.minimum((b * nt + t + 1) * hb, n // HALO - 1), 0))
    return pl.pallas_call(
        _merge_kernel,
        grid=(bn, nt),
        in_specs=[tile(D_MODEL), tile(NA_WIDTH), tile(1536), halo_prev, halo_next, tile(3 * D_MODEL),
                  _resident(scw.shape), _resident(cfw.shape), _resident(cfb.shape), _resident(cfg.shape),
                  _resident(cfbeta.shape), _resident(wb.shape), _resident(wo.shape)],
        out_specs=tile(D_MODEL),
        out_shape=jax.ShapeDtypeStruct((n, D_MODEL), F32),
        scratch_shapes=[pltpu.VMEM((TM + 2 * HALO, SC_WIDTH), F32), pltpu.VMEM((TM + 2 * HALO, CF_WIDTH), F32),
                        pltpu.VMEM((TM, SC_WIDTH), BF16), pltpu.VMEM((TM, CF_WIDTH), BF16)],
        compiler_params=_cparams("parallel", "parallel"),
        name="merge",
    )(x, ya, scu, scu, scu, gates, scw, cfw, cfb, cfg, cfbeta, wb, wo)


def _first_index(hit, n):
    idx = jnp.full(hit[0].shape, n, jnp.int32)
    for i in range(n - 1, -1, -1):
        idx = jnp.where(hit[i], i, idx)
    return idx


def _router_kernel(x_ref, g_ref, wr_ref, br_ref, tri_ref, rows_ref, cls_ref, rank_ref, cnt_ref, carry):
    @pl.when(pl.program_id(0) == 0)
    def _():
        carry[...] = jnp.zeros_like(carry)

    x = x_ref[...]
    ms = jnp.mean(x * x, axis=-1, keepdims=True)
    h = x * lax.rsqrt(ms + EPS) * g_ref[...]
    rows_ref[:, 0:D_MODEL] = h
    logits = lax.dot_general(wr_ref[...], h, (((1,), (1,)), ((), ())), preferred_element_type=F32,
                             precision=lax.Precision.HIGHEST) + br_ref[:, 0:1]
    gl = [logits[g:g + 1, :] for g in range(MOE_GROUPS)]
    gmax = functools.reduce(jnp.maximum, gl)
    gidx = _first_index([gl[g] == gmax for g in range(MOE_GROUPS)], MOE_GROUPS)
    gval = 1.0 / functools.reduce(lambda a, b: a + b, [jnp.exp(v - gmax) for v in gl])
    esel = []
    for j in range(MOE_EPG):
        v = jnp.zeros_like(gmax)
        for g in range(MOE_GROUPS):
            row = MOE_GROUPS + g * MOE_EPG + j
            v = jnp.where(gidx == g, logits[row:row + 1, :], v)
        esel.append(v)
    v1 = functools.reduce(jnp.maximum, esel)
    i1 = _first_index([esel[j] == v1 for j in range(MOE_EPG)], MOE_EPG)
    rest = [jnp.where(i1 == j, -jnp.inf, esel[j]) for j in range(MOE_EPG)]
    v2 = functools.reduce(jnp.maximum, rest)
    i2 = _first_index([(rest[j] == v2) & (i1 != j) for j in range(MOE_EPG)], MOE_EPG)
    e2 = jnp.exp(v2 - v1)
    w1 = gval / (1.0 + e2)
    w2 = gval * e2 / (1.0 + e2)
    a = jnp.minimum(i1, i2)
    b = jnp.maximum(i1, i2)
    wa = jnp.where(i1 < i2, w1, w2)
    wb = jnp.where(i1 < i2, w2, w1)
    cls = gidx * N_PAIRS + (a * (2 * MOE_EPG - 1 - a)) // 2 + (b - a - 1)
    cls_ref[0] = cls

    crow = lax.broadcasted_iota(jnp.int32, (CLS_ROWS, TM), 0)
    onehot = (crow == cls).astype(F32)
    prefix = jnp.dot(onehot.astype(BF16), tri_ref[...], preferred_element_type=F32)
    before = carry[:, 0:1]
    rank = jnp.sum(onehot * (prefix - 1.0 + before), axis=0, keepdims=True)
    rank_ref[0] = rank.astype(jnp.int32)
    carry[...] = carry[...] + jnp.sum(onehot, axis=1, keepdims=True)
    cnt_ref[...] = carry[...]

    wrow = lax.broadcasted_iota(jnp.int32, (128, TM), 0)
    wmat = jnp.where(wrow == 0, wa, jnp.where(wrow == 1, wb, 0.0))
    rows_ref[:, D_MODEL:] = wmat.T


def _router(x, g, wr, br, tri):
    n = x.shape[0]
    nt = n // TM
    return pl.pallas_call(
        _router_kernel,
        grid=(nt,),
        in_specs=[pl.BlockSpec((TM, D_MODEL), lambda i: (i, 0)), _resident((1, D_MODEL)),
                  _resident(wr.shape), _resident(br.shape), _resident(tri.shape)],
        out_specs=[pl.BlockSpec((TM, ROW_W), lambda i: (i, 0)),
                   pl.BlockSpec((1, 1, TM), lambda i: (i, 0, 0)),
                   pl.BlockSpec((1, 1, TM), lambda i: (i, 0, 0)),
                   pl.BlockSpec((CLS_ROWS, 128), lambda i: (0, 0))],
        out_shape=[jax.ShapeDtypeStruct((n, ROW_W), F32),
                   jax.ShapeDtypeStruct((nt, 1, TM), jnp.int32),
                   jax.ShapeDtypeStruct((nt, 1, TM), jnp.int32),
                   jax.ShapeDtypeStruct((CLS_ROWS, 128), F32)],
        scratch_shapes=[pltpu.VMEM((CLS_ROWS, 128), F32)],
        compiler_params=_cparams("arbitrary"),
        name="router",
    )(x, g, wr, br, tri)


def _scatter_kernel(seg_ref, cls_ref, rank_ref, rows_ref, hs_in_ref, hs_ref, sem):
    del hs_in_ref

    def copy(j):
        dst = seg_ref[cls_ref[0, 0, j]] + rank_ref[0, 0, j]
        return pltpu.make_async_copy(rows_ref.at[pl.ds(j, 1)], hs_ref.at[pl.ds(dst, 1)], sem)

    def start(j, c):
        copy(j).start()
        return c

    def wait(j, c):
        copy(j).wait()
        return c

    lax.fori_loop(0, TM, start, 0)
    lax.fori_loop(0, TM, wait, 0)


def _scatter(seg_start, cls, rank, rows, n_sorted):
    n = rows.shape[0]
    nt = n // TM
    hs0 = jnp.zeros((n_sorted, ROW_W), F32)
    smem_tile = pl.BlockSpec((1, 1, TM), lambda i, seg: (i, 0, 0), memory_space=pltpu.SMEM)
    return pl.pallas_call(
        _scatter_kernel,
        grid_spec=pltpu.PrefetchScalarGridSpec(
            num_scalar_prefetch=1,
            grid=(nt,),
            in_specs=[smem_tile, smem_tile,
                      pl.BlockSpec((TM, ROW_W), lambda i, seg: (i, 0)),
                      pl.BlockSpec(memory_space=pl.ANY)],
            out_specs=pl.BlockSpec(memory_space=pl.ANY),
            scratch_shapes=[pltpu.SemaphoreType.DMA(())],
        ),
        out_shape=jax.ShapeDtypeStruct((n_sorted, ROW_W), F32),
        input_output_aliases={4: 0},
        compiler_params=_cparams("arbitrary"),
        name="row_scatter",
    )(seg_start, cls, rank, rows, hs0)


def _gather_kernel(seg_ref, cls_ref, rank_ref, x_ref, ys_ref, o_ref, buf, sem):
    def copy(j):
        src = seg_ref[cls_ref[0, 0, j]] + rank_ref[0, 0, j]
        return pltpu.make_async_copy(ys_ref.at[pl.ds(src, 1)], buf.at[pl.ds(j, 1)], sem)

    def start(j, c):
        copy(j).start()
        return c

    def wait(j, c):
        copy(j).wait()
        return c

    lax.fori_loop(0, TM, start, 0)
    lax.fori_loop(0, TM, wait, 0)
    o_ref[...] = x_ref[...] + buf[...]


def _gather_residual(seg_start, cls, rank, x, ys):
    n = x.shape[0]
    nt = n // TM
    smem_tile = pl.BlockSpec((1, 1, TM), lambda i, seg: (i, 0, 0), memory_space=pltpu.SMEM)
    return pl.pallas_call(
        _gather_kernel,
        grid_spec=pltpu.PrefetchScalarGridSpec(
            num_scalar_prefetch=1,
            grid=(nt,),
            in_specs=[smem_tile, smem_tile,
                      pl.BlockSpec((TM, D_MODEL), lambda i, seg: (i, 0)),
                      pl.BlockSpec(memory_space=pl.ANY)],
            out_specs=pl.BlockSpec((TM, D_MODEL), lambda i, seg: (i, 0)),
            scratch_shapes=[pltpu.VMEM((TM, D_MODEL), F32), pltpu.SemaphoreType.DMA(())],
        ),
        out_shape=jax.ShapeDtypeStruct((n, D_MODEL), F32),
        compiler_params=_cparams("arbitrary"),
        name="row_gather",
    )(seg_start, cls, rank, x, ys)


def _expert_kernel(e1_ref, e2_ref, valid_ref, hs_ref, wgu1_ref, wgu2_ref, wd1_ref, wd2_ref, ys_ref):
    i = pl.program_id(0)

    @pl.when(valid_ref[i] != 0)
    def _():
        h = hs_ref[:, 0:D_MODEL].astype(BF16)
        wts = hs_ref[:, D_MODEL:]

        def expert(wgu_ref, wd_ref):
            gu = jnp.dot(h, wgu_ref[0], preferred_element_type=F32)
            gate = gu[:, 0:D_EXPERT]
            act = gate * jax.nn.sigmoid(gate) * gu[:, D_EXPERT:]
            return jnp.dot(act.astype(BF16), wd_ref[0], preferred_element_type=F32)

        ys_ref[...] = wts[:, 0:1] * expert(wgu1_ref, wd1_ref) + wts[:, 1:2] * expert(wgu2_ref, wd2_ref)

    @pl.when(valid_ref[i] == 0)
    def _():
        ys_ref[...] = jnp.zeros_like(ys_ref)


def _experts(tile_e1, tile_e2, tile_valid, hs, wgu, wd):
    n_sorted = hs.shape[0]
    nt = n_sorted // TME
    return pl.pallas_call(
        _expert_kernel,
        grid_spec=pltpu.PrefetchScalarGridSpec(
            num_scalar_prefetch=3,
            grid=(nt,),
            in_specs=[
                pl.BlockSpec((TME, ROW_W), lambda i, e1, e2, v: (i, 0)),
                pl.BlockSpec((1, D_MODEL, 2 * D_EXPERT), lambda i, e1, e2, v: (e1[i], 0, 0)),
                pl.BlockSpec((1, D_MODEL, 2 * D_EXPERT), lambda i, e1, e2, v: (e2[i], 0, 0)),
                pl.BlockSpec((1, D_EXPERT, D_MODEL), lambda i, e1, e2, v: (e1[i], 0, 0)),
                pl.BlockSpec((1, D_EXPERT, D_MODEL), lambda i, e1, e2, v: (e2[i], 0, 0)),
            ],
            out_specs=pl.BlockSpec((TME, D_MODEL), lambda i, e1, e2, v: (i, 0)),
        ),
        out_shape=jax.ShapeDtypeStruct((n_sorted, D_MODEL), F32),
        compiler_params=_cparams("arbitrary"),
        name="experts",
    )(tile_e1, tile_e2, tile_valid, hs, wgu, wgu, wd, wd)


_PAIR_A = np.array([a for a in range(MOE_EPG) for b in range(a + 1, MOE_EPG)], np.int32)
_PAIR_B = np.array([b for a in range(MOE_EPG) for b in range(a + 1, MOE_EPG)], np.int32)


def _moe(x, lp):
    n = x.shape[0]
    rows, cls, rank, counts = _router(x, lp["norm_ffn_g"], lp["wr"], lp["br"], lp["tri"])
    n_tiles = n // TME + N_CLASSES
    cnt = counts[:N_CLASSES, 0].astype(jnp.int32)
    tiles_per_class = (cnt + TME - 1) // TME
    tile_end = jnp.cumsum(tiles_per_class)
    seg_start = (tile_end - tiles_per_class) * TME
    tile_id = jnp.arange(n_tiles, dtype=jnp.int32)
    tile_cls = jnp.minimum(jnp.searchsorted(tile_end, tile_id, side="right"), N_CLASSES - 1).astype(jnp.int32)
    tile_valid = (tile_id < tile_end[-1]).astype(jnp.int32)
    group = tile_cls // N_PAIRS
    pair = tile_cls % N_PAIRS
    tile_e1 = group * MOE_EPG + jnp.asarray(_PAIR_A)[pair]
    tile_e2 = group * MOE_EPG + jnp.asarray(_PAIR_B)[pair]
    hs = _scatter(seg_start, cls, rank, rows, n_tiles * TME)
    ys = _experts(tile_e1, tile_e2, tile_valid, hs, lp["wgu"], lp["wd"])
    return _gather_residual(seg_start, cls, rank, x, ys)


def _prep_layer(l, norm_mix_g, w_in, b_in, q_norm_g, k_norm_g, na_rpb, sc_conv_w, cf_conv_w, cf_conv_b,
                cf_norm_g, cf_norm_b, w_branch, w_out, norm_ffn_g, router_group_w, router_group_b,
                router_expert_w, router_expert_b, expert_w_gate, expert_w_up, expert_w_down):
    head = np.arange(NA_WIDTH) // NA_HEAD_DIM
    bd = jnp.asarray((head[:, None] == head[None, :]).astype(np.float32) / NA_HEAD_DIM, BF16)
    wr = jnp.zeros((CLS_ROWS, D_MODEL), F32)
    wr = wr.at[0:MOE_GROUPS].set(router_group_w[l].T.astype(F32))
    wr = wr.at[MOE_GROUPS:MOE_GROUPS + MOE_EXPERTS].set(router_expert_w[l].T.astype(F32))
    br = jnp.zeros((CLS_ROWS,), F32)
    br = br.at[0:MOE_GROUPS].set(router_group_b[l].astype(F32))
    br = br.at[MOE_GROUPS:MOE_GROUPS + MOE_EXPERTS].set(router_expert_b[l].astype(F32))
    tri = jnp.asarray(np.triu(np.ones((TM, TM), np.float32)), BF16)
    return dict(
        norm_mix_g=norm_mix_g[l].reshape(1, D_MODEL).astype(F32),
        w_in=w_in[l].astype(BF16),
        b_in=b_in[l].reshape(1, IN_COLS).astype(F32),
        qg=(jnp.tile(q_norm_g[l].astype(F32), NA_HEADS) * (NA_HEAD_DIM ** -0.5)).reshape(1, NA_WIDTH),
        kg=jnp.tile(k_norm_g[l].astype(F32), NA_HEADS).reshape(1, NA_WIDTH),
        bd=bd,
        tbias=_bias_table(na_rpb[l]),
        scw=sc_conv_w[l].astype(F32),
        cfw=cf_conv_w[l].astype(F32),
        cfb=cf_conv_b[l].reshape(1, CF_WIDTH).astype(F32),
        cfg=cf_norm_g[l].reshape(1, CF_WIDTH).astype(F32),
        cfbeta=cf_norm_b[l].reshape(1, CF_WIDTH).astype(F32),
        wb=w_branch[l].astype(BF16),
        wo=w_out[l].astype(BF16),
        norm_ffn_g=norm_ffn_g[l].reshape(1, D_MODEL).astype(F32),
        wr=wr,
        br=jnp.broadcast_to(br[:, None], (CLS_ROWS, 128)),
        tri=tri,
        wgu=jnp.concatenate([expert_w_gate[l], expert_w_up[l]], axis=-1).astype(BF16),
        wd=expert_w_down[l].astype(BF16),
    )


def _trunk(x, layers):
    bn, seq_len, _ = x.shape
    assert seq_len % TM == 0 and seq_len % (RB * GRID_W) == 0
    xf = x.reshape(bn * seq_len, D_MODEL)
    for lp in layers:
        qkv, scu, gates = _inproj(xf, lp["norm_mix_g"], lp["w_in"], lp["b_in"], lp["qg"], lp["kg"], lp["bd"])
        ya = _attention(qkv, bn, seq_len, lp["tbias"])
        xf = _merge(xf, ya, scu, gates, bn, seq_len, lp["scw"], lp["cfw"], lp["cfb"], lp["cfg"], lp["cfbeta"],
                    lp["wb"], lp["wo"])
        xf = _moe(xf, lp)
    return xf.reshape(bn, seq_len, D_MODEL)


def kernel(x_prompt, x_sample, norm_mix_g, w_in, b_in, q_norm_g, k_norm_g, na_rpb, sc_conv_w, cf_conv_w, cf_conv_b, cf_norm_g, cf_norm_b, w_branch, w_out, norm_ffn_g, router_group_w, router_group_b, router_expert_w, router_expert_b, expert_w_gate, expert_w_up, expert_w_down):
    params = (norm_mix_g, w_in, b_in, q_norm_g, k_norm_g, na_rpb, sc_conv_w, cf_conv_w, cf_conv_b, cf_norm_g,
              cf_norm_b, w_branch, w_out, norm_ffn_g, router_group_w, router_group_b, router_expert_w,
              router_expert_b, expert_w_gate, expert_w_up, expert_w_down)
    layers = [_prep_layer(l, *params) for l in range(norm_mix_g.shape[0])]
    return (_trunk(x_prompt, layers), _trunk(x_sample, layers))
```

```python
import functools

import numpy as np
import jax
import jax.numpy as jnp
from jax import lax
from jax.experimental import pallas as pl
from jax.experimental.pallas import tpu as pltpu

F32 = jnp.float32
BF16 = jnp.bfloat16

D_MODEL = 1024
GRID_W = 64
NA_HEADS = 8
NA_HEAD_DIM = 64
NA_WIDTH = NA_HEADS * NA_HEAD_DIM
NA_KH = 8
NA_KW = 16
SC_WIDTH = 512
SC_KERNEL = 3
CF_WIDTH = 512
CF_KERNEL = 31
IN_COLS = 3 * NA_WIDTH + 3 * SC_WIDTH + 2 * CF_WIDTH + 3 * D_MODEL
MOE_GROUPS = 4
MOE_EPG = 4
MOE_EXPERTS = MOE_GROUPS * MOE_EPG
D_EXPERT = 256
EPS = 1e-6

N_PAIRS = MOE_EPG * (MOE_EPG - 1) // 2
N_CLASSES = MOE_GROUPS * N_PAIRS
CLS_ROWS = 32
BLOCK_ROWS = 8
W_ROW = 4
HIGH_HALF = -65536
DMA_UNROLL = 8
MASK_VALUE = -1e30

TM = 512
RB = 8
ATTN_ROWS = 4
HALO = 16
CONV_CHUNK = 128
SUBLANES = 8
TME = 256
VMEM_LIMIT = 56 * 1024 * 1024


def _cparams(*sem):
    return pltpu.CompilerParams(dimension_semantics=sem, vmem_limit_bytes=VMEM_LIMIT)


def _resident(shape):
    nd = len(shape)
    return pl.BlockSpec(shape, lambda *_: (0,) * nd, pipeline_mode=pl.Buffered(1))


def _inproj_kernel(x_ref, g_ref, w_ref, b_ref, qg_ref, kg_ref, bd_ref, qkv_ref, scu_ref, gates_ref):
    x = x_ref[...]
    ms = jnp.mean(x * x, axis=-1, keepdims=True)
    h = (x * lax.rsqrt(ms + EPS) * g_ref[...]).astype(BF16)

    def zcols(c0, c1):
        return jnp.dot(h, w_ref[:, c0:c1], preferred_element_type=F32) + b_ref[:, c0:c1]

    def head_norm(z, gain_ref):
        hm = jnp.dot((z * z).astype(BF16), bd_ref[...], preferred_element_type=F32)
        return z * lax.rsqrt(hm + EPS) * gain_ref[...]

    w = NA_WIDTH
    qkv_ref[:, 0:w] = head_norm(zcols(0, w), qg_ref).astype(BF16)
    qkv_ref[:, w:2 * w] = head_norm(zcols(w, 2 * w), kg_ref).astype(BF16)
    qkv_ref[:, 2 * w:3 * w] = zcols(2 * w, 3 * w).astype(BF16)
    o = 3 * w
    sc_x = zcols(o, o + 512)
    sc_b = zcols(o + 512, o + 1024)
    sc_c = zcols(o + 1024, o + 1536)
    scu_ref[:, 0:512] = (sc_c * sc_x).astype(BF16)
    scu_ref[:, 512:1024] = sc_b.astype(BF16)
    cf_a = zcols(o + 1536, o + 2048)
    cf_gate = zcols(o + 2048, o + 2560)
    scu_ref[:, 1024:1536] = (cf_a * jax.nn.sigmoid(cf_gate)).astype(BF16)
    o = o + 2560
    for c in range(0, 3 * D_MODEL, 512):
        gates_ref[:, c:c + 512] = jax.nn.sigmoid(zcols(o + c, o + c + 512)).astype(BF16)


def _inproj(x, g, w, b, qg, kg, bd):
    n = x.shape[0]
    return pl.pallas_call(
        _inproj_kernel,
        grid=(n // TM,),
        in_specs=[
            pl.BlockSpec((TM, D_MODEL), lambda i: (i, 0)),
            _resident((1, D_MODEL)),
            _resident((D_MODEL, IN_COLS)),
            _resident((1, IN_COLS)),
            _resident((1, NA_WIDTH)),
            _resident((1, NA_WIDTH)),
            _resident((NA_WIDTH, NA_WIDTH)),
        ],
        out_specs=[
            pl.BlockSpec((TM, 3 * NA_WIDTH), lambda i: (i, 0)),
            pl.BlockSpec((TM, 1536), lambda i: (i, 0)),
            pl.BlockSpec((TM, 3 * D_MODEL), lambda i: (i, 0)),
        ],
        out_shape=[
            jax.ShapeDtypeStruct((n, 3 * NA_WIDTH), BF16),
            jax.ShapeDtypeStruct((n, 1536), BF16),
            jax.ShapeDtypeStruct((n, 3 * D_MODEL), BF16),
        ],
        compiler_params=_cparams("parallel"),
        name="inproj",
    )(x, g, w, b, qg, kg, bd)


def _attn_kernel(rows, q_ref, kp_ref, kc_ref, kn_ref, vp_ref, vc_ref, vn_ref, t_ref, o_ref, kbuf, vbuf,
                 s_scr, p_scr):
    blk = RB * GRID_W
    kbuf[0:blk] = kp_ref[...]
    kbuf[blk:2 * blk] = kc_ref[...]
    kbuf[2 * blk:3 * blk] = kn_ref[...]
    vbuf[0:blk] = vp_ref[...]
    vbuf[blk:2 * blk] = vc_ref[...]
    vbuf[2 * blk:3 * blk] = vn_ref[...]
    r0 = pl.program_id(1) * RB
    lane = lax.broadcasted_iota(jnp.int32, (GRID_W, 128), 1)
    low = lane < NA_HEAD_DIM
    nkeys = NA_KH * GRID_W

    npair = NA_HEADS // 2

    def rows_body(ib, carry):
        geo = []
        for rr in range(ATTN_ROWS):
            i = ib * ATTN_ROWS + rr
            r = r0 + i
            rs = jnp.clip(r - NA_KH // 2, 0, rows - NA_KH)
            geo.append((r - rs, pl.multiple_of((rs - r0 + RB) * GRID_W, GRID_W), pl.multiple_of(i * GRID_W, GRID_W)))
        for rr, (d, start, qrow) in enumerate(geo):
            for j in range(npair):
                cols = slice(128 * j, 128 * (j + 1))
                qp = q_ref[pl.ds(qrow, GRID_W), cols]
                zero = jnp.zeros_like(qp)
                q2 = jnp.concatenate([jnp.where(low, qp, zero), jnp.where(low, zero, qp)], axis=0)
                kw = kbuf[pl.ds(start, nkeys), cols]
                s = lax.dot_general(q2, kw, (((1,), (1,)), ((), ())), preferred_element_type=F32)
                s_scr[rr * npair + j] = s + t_ref[j, d]
        for u in range(ATTN_ROWS * npair):
            s = s_scr[u]
            p = jnp.exp(s - jnp.max(s, axis=-1, keepdims=True))
            inv = 1.0 / jnp.sum(p, axis=-1, keepdims=True)
            p_scr[u] = (p * inv).astype(BF16)
        for rr, (d, start, qrow) in enumerate(geo):
            for j in range(npair):
                cols = slice(128 * j, 128 * (j + 1))
                vw = vbuf[pl.ds(start, nkeys), cols]
                o2 = jnp.dot(p_scr[rr * npair + j], vw, preferred_element_type=F32)
                o_ref[pl.ds(qrow, GRID_W), cols] = jnp.where(low, o2[0:GRID_W], o2[GRID_W:]).astype(BF16)
        return carry

    lax.fori_loop(0, RB // ATTN_ROWS, rows_body, 0)


def _attention(qkv, bn, seq_len, tbias):
    rows = seq_len // GRID_W
    assert rows >= NA_KH and rows % RB == 0 and RB >= NA_KH // 2
    nrb = rows // RB
    blk = RB * GRID_W
    n = bn * seq_len

    def at(col, shift):
        def imap(b, i):
            return (b * nrb + jnp.clip(i + shift, 0, nrb - 1), col)
        return pl.BlockSpec((blk, NA_WIDTH), imap)

    return pl.pallas_call(
        functools.partial(_attn_kernel, rows),
        grid=(bn, nrb),
        in_specs=[at(0, 0), at(1, -1), at(1, 0), at(1, 1), at(2, -1), at(2, 0), at(2, 1),
                  _resident(tbias.shape)],
        out_specs=pl.BlockSpec((blk, NA_WIDTH), lambda b, i: (b * nrb + i, 0)),
        out_shape=jax.ShapeDtypeStruct((n, NA_WIDTH), BF16),
        scratch_shapes=[pltpu.VMEM((3 * blk, NA_WIDTH), BF16), pltpu.VMEM((3 * blk, NA_WIDTH), BF16),
                        pltpu.VMEM((ATTN_ROWS * NA_HEADS // 2, 2 * GRID_W, NA_KH * GRID_W), F32),
                        pltpu.VMEM((ATTN_ROWS * NA_HEADS // 2, 2 * GRID_W, NA_KH * GRID_W), BF16)],
        compiler_params=_cparams("parallel", "parallel"),
        name="nattn",
    )(qkv, qkv, qkv, qkv, qkv, qkv, qkv, tbias)


def _bias_table(rpb):
    cols = np.arange(GRID_W)
    cs = np.clip(cols - NA_KW // 2, 0, GRID_W - NA_KW)
    kc = np.arange(GRID_W)
    inside = (kc[None, :] >= cs[:, None]) & (kc[None, :] < cs[:, None] + NA_KW)
    co = kc[None, :] - cols[:, None] + NA_KW - 1
    sel = (co[:, :, None] == np.arange(2 * NA_KW - 1)[None, None, :]) & inside[:, :, None]
    t = jnp.einsum("hrx,ckx->hrck", rpb.astype(F32), jnp.asarray(sel, F32), precision=lax.Precision.HIGHEST)
    t = jnp.where(jnp.asarray(inside)[None, None], t, MASK_VALUE)
    t = jnp.stack([t[:, NA_KH - 1 - d:2 * NA_KH - 1 - d] for d in range(NA_KH)], axis=1)
    t = jnp.transpose(t, (0, 1, 3, 2, 4)).reshape(NA_HEADS // 2, 2, NA_KH, GRID_W, NA_KH * GRID_W)
    return jnp.transpose(t, (0, 2, 1, 3, 4)).reshape(NA_HEADS // 2, NA_KH, 2 * GRID_W, NA_KH * GRID_W)


def _merge_kernel(x_ref, ya_ref, scu_ref, hp_ref, hn_ref, gates_ref, scw_ref, cfw_ref, cfb_ref, cfg_ref,
                  cfbeta_ref, wb_ref, wo_ref, o_ref, ext_s, ext_u, cacc, yb_buf, yc_buf):
    t = pl.program_id(1)
    nt = pl.num_programs(1)
    keep_p = (t > 0).astype(F32)
    keep_n = (t < nt - 1).astype(F32)
    ext_s[0:HALO] = hp_ref[:, 0:512].astype(F32) * keep_p
    ext_s[HALO:HALO + TM] = scu_ref[:, 0:512].astype(F32)
    ext_s[HALO + TM:] = hn_ref[:, 0:512].astype(F32) * keep_n
    ext_u[0:HALO] = hp_ref[:, 1024:1536].astype(F32) * keep_p
    ext_u[HALO:HALO + TM] = scu_ref[:, 1024:1536].astype(F32)
    ext_u[HALO + TM:] = hn_ref[:, 1024:1536].astype(F32) * keep_n

    ch = CONV_CHUNK
    pad = CF_KERNEL // 2
    for cb in range(CF_WIDTH // 128):
        lanes = slice(128 * cb, 128 * (cb + 1))
        for ci in range(TM // ch):
            base = ci * ch
            acc = None
            for s in range(SUBLANES):
                ps = None
                for a in range((CF_KERNEL - s + SUBLANES - 1) // SUBLANES):
                    k = SUBLANES * a + s
                    lo = base + SUBLANES * a
                    term = ext_u[lo:lo + ch + SUBLANES, lanes] * cfw_ref[k:k + 1, lanes]
                    ps = term if ps is None else ps + term
                off = HALO - pad + s
                piece = ps[off:off + ch]
                acc = piece if acc is None else acc + piece
            cacc[base:base + ch, lanes] = acc
            win = ext_s[base + HALO - SUBLANES:base + HALO + ch + SUBLANES, lanes]
            accb = None
            for k in range(SC_KERNEL):
                off = SUBLANES - SC_KERNEL // 2 + k
                term = win[off:off + ch] * scw_ref[k:k + 1, lanes]
                accb = term if accb is None else accb + term
            scb = scu_ref[base:base + ch, 512 + 128 * cb:512 + 128 * (cb + 1)].astype(F32)
            yb_buf[base:base + ch, lanes] = (scb * accb).astype(BF16)

    for ci in range(TM // ch):
        rows = slice(ci * ch, (ci + 1) * ch)
        acc = cacc[rows, :] + cfb_ref[...]
        mu = jnp.mean(acc, axis=-1, keepdims=True)
        xc = acc - mu
        var = jnp.mean(xc * xc, axis=-1, keepdims=True)
        y = xc * lax.rsqrt(var + EPS) * cfg_ref[...] + cfbeta_ref[...]
        yc_buf[rows, :] = (y * jax.nn.sigmoid(y)).astype(BF16)

    merged = gates_ref[:, 0:D_MODEL].astype(F32) * jnp.dot(ya_ref[...], wb_ref[0], preferred_element_type=F32)
    merged = merged + gates_ref[:, D_MODEL:2 * D_MODEL].astype(F32) * jnp.dot(
        yb_buf[...], wb_ref[1], preferred_element_type=F32)
    merged = merged + gates_ref[:, 2 * D_MODEL:].astype(F32) * jnp.dot(
        yc_buf[...], wb_ref[2], preferred_element_type=F32)
    o_ref[...] = x_ref[...] + jnp.dot(merged.astype(BF16), wo_ref[...], preferred_element_type=F32)


def _merge(x, ya, scu, gates, bn, seq_len, scw, cfw, cfb, cfg, cfbeta, wb, wo):
    n = bn * seq_len
    nt = seq_len // TM
    hb = TM // HALO

    def tile(width):
        return pl.BlockSpec((TM, width), lambda b, t: (b * nt + t, 0))

    halo_prev = pl.BlockSpec((HALO, 1536), lambda b, t: (jnp.maximum((b * nt + t) * hb - 1, 0), 0))
    halo_next = pl.BlockSpec((HALO, 1536), lambda b, t: (jnp.minimum((b * nt + t + 1) * hb, n // HALO - 1), 0))
    return pl.pallas_call(
        _merge_kernel,
        grid=(bn, nt),
        in_specs=[tile(D_MODEL), tile(NA_WIDTH), tile(1536), halo_prev, halo_next, tile(3 * D_MODEL),
                  _resident(scw.shape), _resident(cfw.shape), _resident(cfb.shape), _resident(cfg.shape),
                  _resident(cfbeta.shape), _resident(wb.shape), _resident(wo.shape)],
        out_specs=tile(D_MODEL),
        out_shape=jax.ShapeDtypeStruct((n, D_MODEL), F32),
        scratch_shapes=[pltpu.VMEM((TM + 2 * HALO, SC_WIDTH), F32), pltpu.VMEM((TM + 2 * HALO, CF_WIDTH), F32),
                        pltpu.VMEM((TM, CF_WIDTH), F32),
                        pltpu.VMEM((TM, SC_WIDTH), BF16), pltpu.VMEM((TM, CF_WIDTH), BF16)],
        compiler_params=_cparams("parallel", "parallel"),
        name="merge",
    )(x, ya, scu, scu, scu, gates, scw, cfw, cfb, cfg, cfbeta, wb, wo)


def _first_index(hit, n):
    idx = jnp.full(hit[0].shape, n, jnp.int32)
    for i in range(n - 1, -1, -1):
        idx = jnp.where(hit[i], i, idx)
    return idx


def _router_kernel(x_ref, g_ref, wr_ref, br_ref, tri_ref, rows_ref, cls_ref, rank_ref, cnt_ref, carry):
    @pl.when(pl.program_id(0) == 0)
    def _():
        carry[...] = jnp.zeros_like(carry)

    x = x_ref[...]
    ms = jnp.mean(x * x, axis=-1, keepdims=True)
    h = x * lax.rsqrt(ms + EPS) * g_ref[...]
    bits = lax.bitcast_convert_type(h.astype(BF16).astype(F32), jnp.int32)
    half = D_MODEL // 2
    for k in range(half // 128):
        lo = lax.shift_right_logical(bits[:, 128 * k:128 * (k + 1)], 16)
        hi = bits[:, half + 128 * k:half + 128 * (k + 1)] & HIGH_HALF
        rows_ref[pl.ds(k, TM, stride=BLOCK_ROWS), :] = hi | lo
    logits = lax.dot_general(wr_ref[...], h, (((1,), (1,)), ((), ())), preferred_element_type=F32,
                             precision=lax.Precision.HIGHEST) + br_ref[:, 0:1]
    gl = [logits[g:g + 1, :] for g in range(MOE_GROUPS)]
    gmax = functools.reduce(jnp.maximum, gl)
    gidx = _first_index([gl[g] == gmax for g in range(MOE_GROUPS)], MOE_GROUPS)
    gval = 1.0 / functools.reduce(lambda a, b: a + b, [jnp.exp(v - gmax) for v in gl])
    esel = []
    for j in range(MOE_EPG):
        v = jnp.zeros_like(gmax)
        for g in range(MOE_GROUPS):
            row = MOE_GROUPS + g * MOE_EPG + j
            v = jnp.where(gidx == g, logits[row:row + 1, :], v)
        esel.append(v)
    v1 = functools.reduce(jnp.maximum, esel)
    i1 = _first_index([esel[j] == v1 for j in range(MOE_EPG)], MOE_EPG)
    rest = [jnp.where(i1 == j, -jnp.inf, esel[j]) for j in range(MOE_EPG)]
    v2 = functools.reduce(jnp.maximum, rest)
    i2 = _first_index([(rest[j] == v2) & (i1 != j) for j in range(MOE_EPG)], MOE_EPG)
    e2 = jnp.exp(v2 - v1)
    w1 = gval / (1.0 + e2)
    w2 = gval * e2 / (1.0 + e2)
    a = jnp.minimum(i1, i2)
    b = jnp.maximum(i1, i2)
    wa = jnp.where(i1 < i2, w1, w2)
    wb = jnp.where(i1 < i2, w2, w1)
    cls = gidx * N_PAIRS + (a * (2 * MOE_EPG - 1 - a)) // 2 + (b - a - 1)
    cls_ref[0] = cls

    crow = lax.broadcasted_iota(jnp.int32, (CLS_ROWS, TM), 0)
    onehot = (crow == cls).astype(F32)
    prefix = jnp.dot(onehot.astype(BF16), tri_ref[...], preferred_element_type=F32)
    before = carry[:, 0:1]
    rank = jnp.sum(onehot * (prefix - 1.0 + before), axis=0, keepdims=True)
    rank_ref[0] = rank.astype(jnp.int32)
    carry[...] = carry[...] + jnp.sum(onehot, axis=1, keepdims=True)
    cnt_ref[...] = carry[...]

    wrow = lax.broadcasted_iota(jnp.int32, (128, TM), 0)
    wmat = jnp.where(wrow == 0, wa, jnp.where(wrow == 1, wb, 0.0))
    rows_ref[pl.ds(W_ROW, TM, stride=BLOCK_ROWS), :] = lax.bitcast_convert_type(wmat.T, jnp.int32)
    for k in range(W_ROW + 1, BLOCK_ROWS):
        rows_ref[pl.ds(k, TM, stride=BLOCK_ROWS), :] = jnp.zeros((TM, 128), jnp.int32)


def _router(x, g, wr, br, tri):
    n = x.shape[0]
    nt = n // TM
    return pl.pallas_call(
        _router_kernel,
        grid=(nt,),
        in_specs=[pl.BlockSpec((TM, D_MODEL), lambda i: (i, 0)), _resident((1, D_MODEL)),
                  _resident(wr.shape), _resident(br.shape), _resident(tri.shape)],
        out_specs=[pl.BlockSpec((TM * BLOCK_ROWS, 128), lambda i: (i, 0)),
                   pl.BlockSpec((1, 1, TM), lambda i: (i, 0, 0)),
                   pl.BlockSpec((1, 1, TM), lambda i: (i, 0, 0)),
                   pl.BlockSpec((CLS_ROWS, 128), lambda i: (0, 0))],
        out_shape=[jax.ShapeDtypeStruct((n * BLOCK_ROWS, 128), jnp.int32),
                   jax.ShapeDtypeStruct((nt, 1, TM), jnp.int32),
                   jax.ShapeDtypeStruct((nt, 1, TM), jnp.int32),
                   jax.ShapeDtypeStruct((CLS_ROWS, 128), F32)],
        scratch_shapes=[pltpu.VMEM((CLS_ROWS, 128), F32)],
        compiler_params=_cparams("arbitrary"),
        name="router",
    )(x, g, wr, br, tri)


def _block(ref, row):
    return ref.at[pl.ds(pl.multiple_of(row * BLOCK_ROWS, BLOCK_ROWS), BLOCK_ROWS)]


def _scatter_kernel(dest_ref, rows_ref, hs_in_ref, hs_ref, sem):
    del hs_in_ref

    def copy(j):
        return pltpu.make_async_copy(_block(rows_ref, j), _block(hs_ref, dest_ref[0, 0, j]), sem)

    def start(j, c):
        copy(j).start()
        return c

    def wait(j, c):
        copy(j).wait()
        return c

    lax.fori_loop(0, TM, start, 0, unroll=DMA_UNROLL)
    lax.fori_loop(0, TM, wait, 0, unroll=DMA_UNROLL)


def _scatter(dest, rows, n_sorted):
    nt = dest.shape[0]
    hs0 = jnp.zeros((n_sorted * BLOCK_ROWS, 128), jnp.int32)
    return pl.pallas_call(
        _scatter_kernel,
        grid=(nt,),
        in_specs=[pl.BlockSpec((1, 1, TM), lambda i: (i, 0, 0), memory_space=pltpu.SMEM),
                  pl.BlockSpec((TM * BLOCK_ROWS, 128), lambda i: (i, 0)),
                  pl.BlockSpec(memory_space=pl.ANY)],
        out_specs=pl.BlockSpec(memory_space=pl.ANY),
        scratch_shapes=[pltpu.SemaphoreType.DMA(())],
        out_shape=jax.ShapeDtypeStruct((n_sorted * BLOCK_ROWS, 128), jnp.int32),
        input_output_aliases={2: 0},
        compiler_params=_cparams("arbitrary"),
        name="row_scatter",
    )(dest, rows, hs0)


def _gather_kernel(dest_ref, x_ref, ys_ref, o_ref, buf, sem):
    def copy(j):
        return pltpu.make_async_copy(_block(ys_ref, dest_ref[0, 0, j]), _block(buf, j), sem)

    def start(j, c):
        copy(j).start()
        return c

    def wait(j, c):
        copy(j).wait()
        return c

    lax.fori_loop(0, TM, start, 0, unroll=DMA_UNROLL)
    lax.fori_loop(0, TM, wait, 0, unroll=DMA_UNROLL)
    for k in range(D_MODEL // 128):
        lanes = slice(128 * k, 128 * (k + 1))
        o_ref[:, lanes] = x_ref[:, lanes] + buf[pl.ds(k, TM, stride=BLOCK_ROWS), :]


def _gather_residual(dest, x, ys):
    n = x.shape[0]
    nt = n // TM
    return pl.pallas_call(
        _gather_kernel,
        grid=(nt,),
        in_specs=[pl.BlockSpec((1, 1, TM), lambda i: (i, 0, 0), memory_space=pltpu.SMEM),
                  pl.BlockSpec((TM, D_MODEL), lambda i: (i, 0)),
                  pl.BlockSpec(memory_space=pl.ANY)],
        out_specs=pl.BlockSpec((TM, D_MODEL), lambda i: (i, 0)),
        scratch_shapes=[pltpu.VMEM((TM * BLOCK_ROWS, 128), F32), pltpu.SemaphoreType.DMA(())],
        out_shape=jax.ShapeDtypeStruct((n, D_MODEL), F32),
        compiler_params=_cparams("arbitrary"),
        name="row_gather",
    )(dest, x, ys)


def _expert_kernel(e1_ref, e2_ref, valid_ref, hs_ref, wgu1_ref, wgu2_ref, wd1_ref, wd2_ref, ys_ref):
    i = pl.program_id(0)

    @pl.when(valid_ref[i] != 0)
    def _():
        words = [hs_ref[pl.ds(k, TME, stride=BLOCK_ROWS), :] for k in range(D_MODEL // 256)]
        lo = [lax.bitcast_convert_type(lax.shift_left(w, 16), F32).astype(BF16) for w in words]
        hi = [lax.bitcast_convert_type(w & HIGH_HALF, F32).astype(BF16) for w in words]
        h = jnp.concatenate(lo + hi, axis=1)
        wts = lax.bitcast_convert_type(hs_ref[pl.ds(W_ROW, TME, stride=BLOCK_ROWS), :], F32)

        def expert(wgu_ref, wd_ref):
            gu = jnp.dot(h, wgu_ref[0], preferred_element_type=F32)
            gate = gu[:, 0:D_EXPERT]
            act = gate * jax.nn.sigmoid(gate) * gu[:, D_EXPERT:]
            return jnp.dot(act.astype(BF16), wd_ref[0], preferred_element_type=F32)

        y = wts[:, 0:1] * expert(wgu1_ref, wd1_ref) + wts[:, 1:2] * expert(wgu2_ref, wd2_ref)
        for k in range(D_MODEL // 128):
            ys_ref[pl.ds(k, TME, stride=BLOCK_ROWS), :] = y[:, 128 * k:128 * (k + 1)]

    @pl.when(valid_ref[i] == 0)
    def _():
        ys_ref[...] = jnp.zeros_like(ys_ref)


def _experts(tile_e1, tile_e2, tile_valid, hs, wgu, wd):
    n_sorted = hs.shape[0] // BLOCK_ROWS
    nt = n_sorted // TME
    return pl.pallas_call(
        _expert_kernel,
        grid_spec=pltpu.PrefetchScalarGridSpec(
            num_scalar_prefetch=3,
            grid=(nt,),
            in_specs=[
                pl.BlockSpec((TME * BLOCK_ROWS, 128), lambda i, e1, e2, v: (i, 0)),
                pl.BlockSpec((1, D_MODEL, 2 * D_EXPERT), lambda i, e1, e2, v: (e1[i], 0, 0)),
                pl.BlockSpec((1, D_MODEL, 2 * D_EXPERT), lambda i, e1, e2, v: (e2[i], 0, 0)),
                pl.BlockSpec((1, D_EXPERT, D_MODEL), lambda i, e1, e2, v: (e1[i], 0, 0)),
                pl.BlockSpec((1, D_EXPERT, D_MODEL), lambda i, e1, e2, v: (e2[i], 0, 0)),
            ],
            out_specs=pl.BlockSpec((TME * BLOCK_ROWS, 128), lambda i, e1, e2, v: (i, 0)),
        ),
        out_shape=jax.ShapeDtypeStruct((n_sorted * BLOCK_ROWS, 128), F32),
        compiler_params=_cparams("arbitrary"),
        name="experts",
    )(tile_e1, tile_e2, tile_valid, hs, wgu, wgu, wd, wd)


_PAIR_A = np.array([a for a in range(MOE_EPG) for b in range(a + 1, MOE_EPG)], np.int32)
_PAIR_B = np.array([b for a in range(MOE_EPG) for b in range(a + 1, MOE_EPG)], np.int32)


def _moe(x, lp):
    n = x.shape[0]
    rows, cls, rank, counts = _router(x, lp["norm_ffn_g"], lp["wr"], lp["br"], lp["tri"])
    n_tiles = n // TME + N_CLASSES
    cnt = counts[:N_CLASSES, 0].astype(jnp.int32)
    tiles_per_class = (cnt + TME - 1) // TME
    tile_end = jnp.cumsum(tiles_per_class)
    seg_start = (tile_end - tiles_per_class) * TME
    tile_id = jnp.arange(n_tiles, dtype=jnp.int32)
    tile_cls = jnp.sum((tile_id[:, None] >= tile_end[None, :]).astype(jnp.int32), axis=1)
    tile_valid = (tile_cls < N_CLASSES).astype(jnp.int32)
    tile_cls = jnp.minimum(tile_cls, N_CLASSES - 1)
    class_id = jnp.arange(N_CLASSES, dtype=jnp.int32)
    class_np = np.arange(N_CLASSES)
    class_e1 = jnp.asarray((class_np // N_PAIRS) * MOE_EPG + _PAIR_A[class_np % N_PAIRS], jnp.int32)
    class_e2 = jnp.asarray((class_np // N_PAIRS) * MOE_EPG + _PAIR_B[class_np % N_PAIRS], jnp.int32)
    tile_hot = (tile_cls[:, None] == class_id[None, :]).astype(jnp.int32)
    tile_e1 = jnp.sum(tile_hot * class_e1[None, :], axis=1)
    tile_e2 = jnp.sum(tile_hot * class_e2[None, :], axis=1)
    dest = rank + jnp.sum(jnp.where(cls[..., None] == class_id, seg_start, 0), axis=-1)
    hs = _scatter(dest, rows, n_tiles * TME)
    ys = _experts(tile_e1, tile_e2, tile_valid, hs, lp["wgu"], lp["wd"])
    return _gather_residual(dest, x, ys)


def _prep_layer(l, norm_mix_g, w_in, b_in, q_norm_g, k_norm_g, na_rpb, sc_conv_w, cf_conv_w, cf_conv_b,
                cf_norm_g, cf_norm_b, w_branch, w_out, norm_ffn_g, router_group_w, router_group_b,
                router_expert_w, router_expert_b, expert_w_gate, expert_w_up, expert_w_down):
    head = np.arange(NA_WIDTH) // NA_HEAD_DIM
    bd = jnp.asarray((head[:, None] == head[None, :]).astype(np.float32) / NA_HEAD_DIM, BF16)
    wr = jnp.zeros((CLS_ROWS, D_MODEL), F32)
    wr = wr.at[0:MOE_GROUPS].set(router_group_w[l].T.astype(F32))
    wr = wr.at[MOE_GROUPS:MOE_GROUPS + MOE_EXPERTS].set(router_expert_w[l].T.astype(F32))
    br = jnp.zeros((CLS_ROWS,), F32)
    br = br.at[0:MOE_GROUPS].set(router_group_b[l].astype(F32))
    br = br.at[MOE_GROUPS:MOE_GROUPS + MOE_EXPERTS].set(router_expert_b[l].astype(F32))
    tri = jnp.asarray(np.triu(np.ones((TM, TM), np.float32)), BF16)
    return dict(
        norm_mix_g=norm_mix_g[l].reshape(1, D_MODEL).astype(F32),
        w_in=w_in[l].astype(BF16),
        b_in=b_in[l].reshape(1, IN_COLS).astype(F32),
        qg=(jnp.tile(q_norm_g[l].astype(F32), NA_HEADS) * (NA_HEAD_DIM ** -0.5)).reshape(1, NA_WIDTH),
        kg=jnp.tile(k_norm_g[l].astype(F32), NA_HEADS).reshape(1, NA_WIDTH),
        bd=bd,
        tbias=_bias_table(na_rpb[l]),
        scw=sc_conv_w[l].astype(F32),
        cfw=cf_conv_w[l].astype(F32),
        cfb=cf_conv_b[l].reshape(1, CF_WIDTH).astype(F32),
        cfg=cf_norm_g[l].reshape(1, CF_WIDTH).astype(F32),
        cfbeta=cf_norm_b[l].reshape(1, CF_WIDTH).astype(F32),
        wb=w_branch[l].astype(BF16),
        wo=w_out[l].astype(BF16),
        norm_ffn_g=norm_ffn_g[l].reshape(1, D_MODEL).astype(F32),
        wr=wr,
        br=jnp.broadcast_to(br[:, None], (CLS_ROWS, 128)),
        tri=tri,
        wgu=jnp.concatenate([expert_w_gate[l], expert_w_up[l]], axis=-1).astype(BF16),
        wd=expert_w_down[l].astype(BF16),
    )


def _trunk(x, layers):
    bn, seq_len, _ = x.shape
    assert seq_len % TM == 0 and seq_len % (RB * GRID_W) == 0
    xf = x.reshape(bn * seq_len, D_MODEL)
    for lp in layers:
        qkv, scu, gates = _inproj(xf, lp["norm_mix_g"], lp["w_in"], lp["b_in"], lp["qg"], lp["kg"], lp["bd"])
        ya = _attention(qkv, bn, seq_len, lp["tbias"])
        xf = _merge(xf, ya, scu, gates, bn, seq_len, lp["scw"], lp["cfw"], lp["cfb"], lp["cfg"], lp["cfbeta"],
                    lp["wb"], lp["wo"])
        xf = _moe(xf, lp)
    return xf.reshape(bn, seq_len, D_MODEL)


def kernel(x_prompt, x_sample, norm_mix_g, w_in, b_in, q_norm_g, k_norm_g, na_rpb, sc_conv_w, cf_conv_w, cf_conv_b, cf_norm_g, cf_norm_b, w_branch, w_out, norm_ffn_g, router_group_w, router_group_b, router_expert_w, router_expert_b, expert_w_gate, expert_w_up, expert_w_down):
    params = (norm_mix_g, w_in, b_in, q_norm_g, k_norm_g, na_rpb, sc_conv_w, cf_conv_w, cf_conv_b, cf_norm_g,
              cf_norm_b, w_branch, w_out, norm_ffn_g, router_group_w, router_group_b, router_expert_w,
              router_expert_b, expert_w_gate, expert_w_up, expert_w_down)
    layers = [_prep_layer(l, *params) for l in range(norm_mix_g.shape[0])]
    return (_trunk(x_prompt, layers), _trunk(x_sample, layers))
```

```python
import functools

import numpy as np
import jax
import jax.numpy as jnp
from jax import lax
from jax.experimental import pallas as pl
from jax.experimental.pallas import tpu as pltpu

F32 = jnp.float32
BF16 = jnp.bfloat16

D_MODEL = 1024
GRID_W = 64
NA_HEADS = 8
NA_HEAD_DIM = 64
NA_WIDTH = NA_HEADS * NA_HEAD_DIM
NA_KH = 8
NA_KW = 16
SC_WIDTH = 512
SC_KERNEL = 3
CF_WIDTH = 512
CF_KERNEL = 31
IN_COLS = 3 * NA_WIDTH + 3 * SC_WIDTH + 2 * CF_WIDTH + 3 * D_MODEL
MOE_GROUPS = 4
MOE_EPG = 4
MOE_EXPERTS = MOE_GROUPS * MOE_EPG
D_EXPERT = 256
EPS = 1e-6

N_PAIRS = MOE_EPG * (MOE_EPG - 1) // 2
N_CLASSES = MOE_GROUPS * N_PAIRS
CLS_ROWS = 32
BLOCK_ROWS = 8
W_ROW = 4
HIGH_HALF = -65536
DMA_UNROLL = 8
MASK_VALUE = -1e30

TM = 512
RB = 8
ATTN_ROWS = 4
HALO = 16
CONV_CHUNK = 128
SUBLANES = 8
TME = 256
VMEM_LIMIT = 56 * 1024 * 1024


def _cparams(*sem):
    return pltpu.CompilerParams(dimension_semantics=sem, vmem_limit_bytes=VMEM_LIMIT)


def _resident(shape):
    nd = len(shape)
    return pl.BlockSpec(shape, lambda *_: (0,) * nd, pipeline_mode=pl.Buffered(1))


def _inproj_kernel(x_ref, g_ref, w_ref, b_ref, qg_ref, kg_ref, bd_ref, qkv_ref, scu_ref, gates_ref):
    x = x_ref[...]
    ms = jnp.mean(x * x, axis=-1, keepdims=True)
    h = (x * lax.rsqrt(ms + EPS) * g_ref[...]).astype(BF16)

    def zcols(c0, c1):
        return jnp.dot(h, w_ref[:, c0:c1], preferred_element_type=F32) + b_ref[:, c0:c1]

    def head_norm(z, gain_ref):
        hm = jnp.dot((z * z).astype(BF16), bd_ref[...], preferred_element_type=F32)
        return z * lax.rsqrt(hm + EPS) * gain_ref[...]

    w = NA_WIDTH
    qkv_ref[:, 0:w] = head_norm(zcols(0, w), qg_ref).astype(BF16)
    qkv_ref[:, w:2 * w] = head_norm(zcols(w, 2 * w), kg_ref).astype(BF16)
    qkv_ref[:, 2 * w:3 * w] = zcols(2 * w, 3 * w).astype(BF16)
    o = 3 * w
    sc_x = zcols(o, o + 512)
    sc_b = zcols(o + 512, o + 1024)
    sc_c = zcols(o + 1024, o + 1536)
    scu_ref[:, 0:512] = (sc_c * sc_x).astype(BF16)
    scu_ref[:, 512:1024] = sc_b.astype(BF16)
    cf_a = zcols(o + 1536, o + 2048)
    cf_gate = zcols(o + 2048, o + 2560)
    scu_ref[:, 1024:1536] = (cf_a * jax.nn.sigmoid(cf_gate)).astype(BF16)
    o = o + 2560
    for c in range(0, 3 * D_MODEL, 512):
        gates_ref[:, c:c + 512] = jax.nn.sigmoid(zcols(o + c, o + c + 512)).astype(BF16)


def _inproj(x, g, w, b, qg, kg, bd):
    n = x.shape[0]
    return pl.pallas_call(
        _inproj_kernel,
        grid=(n // TM,),
        in_specs=[
            pl.BlockSpec((TM, D_MODEL), lambda i: (i, 0)),
            _resident((1, D_MODEL)),
            _resident((D_MODEL, IN_COLS)),
            _resident((1, IN_COLS)),
            _resident((1, NA_WIDTH)),
            _resident((1, NA_WIDTH)),
            _resident((NA_WIDTH, NA_WIDTH)),
        ],
        out_specs=[
            pl.BlockSpec((TM, 3 * NA_WIDTH), lambda i: (i, 0)),
            pl.BlockSpec((TM, 1536), lambda i: (i, 0)),
            pl.BlockSpec((TM, 3 * D_MODEL), lambda i: (i, 0)),
        ],
        out_shape=[
            jax.ShapeDtypeStruct((n, 3 * NA_WIDTH), BF16),
            jax.ShapeDtypeStruct((n, 1536), BF16),
            jax.ShapeDtypeStruct((n, 3 * D_MODEL), BF16),
        ],
        compiler_params=_cparams("parallel"),
        name="inproj",
    )(x, g, w, b, qg, kg, bd)


def _attn_kernel(rows, q_ref, kp_ref, kc_ref, kn_ref, vp_ref, vc_ref, vn_ref, t_ref, o_ref, kbuf, vbuf,
                 s_scr, p_scr):
    blk = RB * GRID_W
    kbuf[0:blk] = kp_ref[...]
    kbuf[blk:2 * blk] = kc_ref[...]
    kbuf[2 * blk:3 * blk] = kn_ref[...]
    vbuf[0:blk] = vp_ref[...]
    vbuf[blk:2 * blk] = vc_ref[...]
    vbuf[2 * blk:3 * blk] = vn_ref[...]
    r0 = pl.program_id(1) * RB
    lane = lax.broadcasted_iota(jnp.int32, (GRID_W, 128), 1)
    low = lane < NA_HEAD_DIM
    nkeys = NA_KH * GRID_W

    npair = NA_HEADS // 2

    def rows_body(ib, carry):
        geo = []
        for rr in range(ATTN_ROWS):
            i = ib * ATTN_ROWS + rr
            r = r0 + i
            rs = jnp.clip(r - NA_KH // 2, 0, rows - NA_KH)
            geo.append((r - rs, pl.multiple_of((rs - r0 + RB) * GRID_W, GRID_W), pl.multiple_of(i * GRID_W, GRID_W)))
        for rr, (d, start, qrow) in enumerate(geo):
            for j in range(npair):
                cols = slice(128 * j, 128 * (j + 1))
                qp = q_ref[pl.ds(qrow, GRID_W), cols]
                zero = jnp.zeros_like(qp)
                q2 = jnp.concatenate([jnp.where(low, qp, zero), jnp.where(low, zero, qp)], axis=0)
                kw = kbuf[pl.ds(start, nkeys), cols]
                s = lax.dot_general(q2, kw, (((1,), (1,)), ((), ())), preferred_element_type=F32)
                s_scr[rr * npair + j] = s + t_ref[j, d]
        for u in range(ATTN_ROWS * npair):
            s = s_scr[u]
            p = jnp.exp(s - jnp.max(s, axis=-1, keepdims=True))
            inv = 1.0 / jnp.sum(p, axis=-1, keepdims=True)
            p_scr[u] = (p * inv).astype(BF16)
        for rr, (d, start, qrow) in enumerate(geo):
            for j in range(npair):
                cols = slice(128 * j, 128 * (j + 1))
                vw = vbuf[pl.ds(start, nkeys), cols]
                o2 = jnp.dot(p_scr[rr * npair + j], vw, preferred_element_type=F32)
                o_ref[pl.ds(qrow, GRID_W), cols] = jnp.where(low, o2[0:GRID_W], o2[GRID_W:]).astype(BF16)
        return carry

    lax.fori_loop(0, RB // ATTN_ROWS, rows_body, 0)


def _attention(qkv, bn, seq_len, tbias):
    rows = seq_len // GRID_W
    assert rows >= NA_KH and rows % RB == 0 and RB >= NA_KH // 2
    nrb = rows // RB
    blk = RB * GRID_W
    n = bn * seq_len

    def at(col, shift):
        def imap(b, i):
            return (b * nrb + jnp.clip(i + shift, 0, nrb - 1), col)
        return pl.BlockSpec((blk, NA_WIDTH), imap)

    return pl.pallas_call(
        functools.partial(_attn_kernel, rows),
        grid=(bn, nrb),
        in_specs=[at(0, 0), at(1, -1), at(1, 0), at(1, 1), at(2, -1), at(2, 0), at(2, 1),
                  _resident(tbias.shape)],
        out_specs=pl.BlockSpec((blk, NA_WIDTH), lambda b, i: (b * nrb + i, 0)),
        out_shape=jax.ShapeDtypeStruct((n, NA_WIDTH), BF16),
        scratch_shapes=[pltpu.VMEM((3 * blk, NA_WIDTH), BF16), pltpu.VMEM((3 * blk, NA_WIDTH), BF16),
                        pltpu.VMEM((ATTN_ROWS * NA_HEADS // 2, 2 * GRID_W, NA_KH * GRID_W), F32),
                        pltpu.VMEM((ATTN_ROWS * NA_HEADS // 2, 2 * GRID_W, NA_KH * GRID_W), BF16)],
        compiler_params=_cparams("parallel", "parallel"),
        name="nattn",
    )(qkv, qkv, qkv, qkv, qkv, qkv, qkv, tbias)


def _bias_table(rpb):
    cols = np.arange(GRID_W)
    cs = np.clip(cols - NA_KW // 2, 0, GRID_W - NA_KW)
    kc = np.arange(GRID_W)
    inside = (kc[None, :] >= cs[:, None]) & (kc[None, :] < cs[:, None] + NA_KW)
    co = kc[None, :] - cols[:, None] + NA_KW - 1
    sel = (co[:, :, None] == np.arange(2 * NA_KW - 1)[None, None, :]) & inside[:, :, None]
    t = jnp.einsum("hrx,ckx->hrck", rpb.astype(F32), jnp.asarray(sel, F32), precision=lax.Precision.HIGHEST)
    t = jnp.where(jnp.asarray(inside)[None, None], t, MASK_VALUE)
    t = jnp.stack([t[:, NA_KH - 1 - d:2 * NA_KH - 1 - d] for d in range(NA_KH)], axis=1)
    t = jnp.transpose(t, (0, 1, 3, 2, 4)).reshape(NA_HEADS // 2, 2, NA_KH, GRID_W, NA_KH * GRID_W)
    return jnp.transpose(t, (0, 2, 1, 3, 4)).reshape(NA_HEADS // 2, NA_KH, 2 * GRID_W, NA_KH * GRID_W)


def _conv_tile(t, nt_seq, scu_ref, hp_ref, hn_ref, scw_ref, cfw_ref, cfb_ref, cfg_ref, cfbeta_ref,
               ext_s, ext_u, cacc, ybc_ref):
    keep_p = t > 0
    keep_n = t < nt_seq - 1
    ext_s[0:HALO] = jnp.where(keep_p, hp_ref[:, 0:512].astype(F32), 0.0)
    ext_s[HALO:HALO + TM] = scu_ref[:, 0:512].astype(F32)
    ext_s[HALO + TM:] = jnp.where(keep_n, hn_ref[:, 0:512].astype(F32), 0.0)
    ext_u[0:HALO] = jnp.where(keep_p, hp_ref[:, 1024:1536].astype(F32), 0.0)
    ext_u[HALO:HALO + TM] = scu_ref[:, 1024:1536].astype(F32)
    ext_u[HALO + TM:] = jnp.where(keep_n, hn_ref[:, 1024:1536].astype(F32), 0.0)

    ch = CONV_CHUNK
    pad = CF_KERNEL // 2
    for cb in range(CF_WIDTH // 128):
        lanes = slice(128 * cb, 128 * (cb + 1))
        for ci in range(TM // ch):
            base = ci * ch
            acc = None
            for s in range(SUBLANES):
                ps = None
                for a in range((CF_KERNEL - s + SUBLANES - 1) // SUBLANES):
                    k = SUBLANES * a + s
                    lo = base + SUBLANES * a
                    term = ext_u[lo:lo + ch + SUBLANES, lanes] * cfw_ref[k:k + 1, lanes]
                    ps = term if ps is None else ps + term
                off = HALO - pad + s
                piece = ps[off:off + ch]
                acc = piece if acc is None else acc + piece
            cacc[base:base + ch, lanes] = acc
            win = ext_s[base + HALO - SUBLANES:base + HALO + ch + SUBLANES, lanes]
            accb = None
            for k in range(SC_KERNEL):
                off = SUBLANES - SC_KERNEL // 2 + k
                term = win[off:off + ch] * scw_ref[k:k + 1, lanes]
                accb = term if accb is None else accb + term
            scb = scu_ref[base:base + ch, 512 + 128 * cb:512 + 128 * (cb + 1)].astype(F32)
            ybc_ref[base:base + ch, lanes] = (scb * accb).astype(BF16)

    for ci in range(TM // ch):
        rows = slice(ci * ch, (ci + 1) * ch)
        acc = cacc[rows, :] + cfb_ref[...]
        mu = jnp.mean(acc, axis=-1, keepdims=True)
        xc = acc - mu
        var = jnp.mean(xc * xc, axis=-1, keepdims=True)
        y = xc * lax.rsqrt(var + EPS) * cfg_ref[...] + cfbeta_ref[...]
        ybc_ref[rows, SC_WIDTH:] = (y * jax.nn.sigmoid(y)).astype(BF16)


def _merge_kernel(nt_seq, x_ref, ya_ref, gates_ref, scu_ref, hp_ref, hn_ref, scw_ref, cfw_ref, cfb_ref, cfg_ref,
                  cfbeta_ref, wb_ref, wo_ref, o_ref, ext_s, ext_u, cacc, ybc):
    _conv_tile(lax.rem(pl.program_id(0), nt_seq), nt_seq, scu_ref, hp_ref, hn_ref, scw_ref, cfw_ref, cfb_ref,
               cfg_ref, cfbeta_ref, ext_s, ext_u, cacc, ybc)
    merged = gates_ref[:, 0:D_MODEL].astype(F32) * jnp.dot(ya_ref[...], wb_ref[0], preferred_element_type=F32)
    merged = merged + gates_ref[:, D_MODEL:2 * D_MODEL].astype(F32) * jnp.dot(
        ybc[:, 0:SC_WIDTH], wb_ref[1], preferred_element_type=F32)
    merged = merged + gates_ref[:, 2 * D_MODEL:].astype(F32) * jnp.dot(
        ybc[:, SC_WIDTH:], wb_ref[2], preferred_element_type=F32)
    o_ref[...] = x_ref[...] + jnp.dot(merged.astype(BF16), wo_ref[...], preferred_element_type=F32)


def _merge(x, ya, scu, gates, seq_len, scw, cfw, cfb, cfg, cfbeta, wb, wo):
    n = x.shape[0]
    hb = TM // HALO

    def tile(width):
        return pl.BlockSpec((TM, width), lambda i: (i, 0))

    halo_prev = pl.BlockSpec((HALO, 1536), lambda i: (jnp.maximum(i * hb - 1, 0), 0))
    halo_next = pl.BlockSpec((HALO, 1536), lambda i: (jnp.minimum((i + 1) * hb, n // HALO - 1), 0))
    return pl.pallas_call(
        functools.partial(_merge_kernel, seq_len // TM),
        grid=(n // TM,),
        in_specs=[tile(D_MODEL), tile(NA_WIDTH), tile(3 * D_MODEL), tile(1536), halo_prev, halo_next,
                  _resident(scw.shape), _resident(cfw.shape), _resident(cfb.shape), _resident(cfg.shape),
                  _resident(cfbeta.shape), _resident(wb.shape), _resident(wo.shape)],
        out_specs=tile(D_MODEL),
        out_shape=jax.ShapeDtypeStruct((n, D_MODEL), F32),
        scratch_shapes=[pltpu.VMEM((TM + 2 * HALO, SC_WIDTH), F32), pltpu.VMEM((TM + 2 * HALO, CF_WIDTH), F32),
                        pltpu.VMEM((TM, CF_WIDTH), F32), pltpu.VMEM((TM, SC_WIDTH + CF_WIDTH), BF16)],
        compiler_params=_cparams("parallel"),
        name="merge",
    )(x, ya, gates, scu, scu, scu, scw, cfw, cfb, cfg, cfbeta, wb, wo)


def _first_index(hit, n):
    idx = jnp.full(hit[0].shape, n, jnp.int32)
    for i in range(n - 1, -1, -1):
        idx = jnp.where(hit[i], i, idx)
    return idx


def _router_kernel(x_ref, g_ref, wr_ref, br_ref, tri_ref, rows_ref, cls_ref, rank_ref, cnt_ref, carry):
    @pl.when(pl.program_id(0) == 0)
    def _():
        carry[...] = jnp.zeros_like(carry)

    x = x_ref[...]
    ms = jnp.mean(x * x, axis=-1, keepdims=True)
    h = x * lax.rsqrt(ms + EPS) * g_ref[...]
    h_hi = h.astype(BF16)
    h_hi32 = h_hi.astype(F32)
    bits = lax.bitcast_convert_type(h_hi32, jnp.int32)
    half = D_MODEL // 2
    for k in range(half // 128):
        lo = lax.shift_right_logical(bits[:, 128 * k:128 * (k + 1)], 16)
        hi = bits[:, half + 128 * k:half + 128 * (k + 1)] & HIGH_HALF
        rows_ref[pl.ds(k, TM, stride=BLOCK_ROWS), :] = hi | lo
    h_lo = (h - h_hi32).astype(BF16)
    nt_dims = (((1,), (1,)), ((), ()))
    part = lax.dot_general(wr_ref[...], h_hi, nt_dims, preferred_element_type=F32)
    logits = (part[0:CLS_ROWS] + part[CLS_ROWS:]
              + lax.dot_general(wr_ref[0:CLS_ROWS, :], h_lo, nt_dims, preferred_element_type=F32)
              + br_ref[:, 0:1])
    gl = [logits[g:g + 1, :] for g in range(MOE_GROUPS)]
    gmax = functools.reduce(jnp.maximum, gl)
    gidx = _first_index([gl[g] == gmax for g in range(MOE_GROUPS)], MOE_GROUPS)
    gval = 1.0 / functools.reduce(lambda a, b: a + b, [jnp.exp(v - gmax) for v in gl])
    esel = []
    for j in range(MOE_EPG):
        v = jnp.zeros_like(gmax)
        for g in range(MOE_GROUPS):
            row = MOE_GROUPS + g * MOE_EPG + j
            v = jnp.where(gidx == g, logits[row:row + 1, :], v)
        esel.append(v)
    v1 = functools.reduce(jnp.maximum, esel)
    i1 = _first_index([esel[j] == v1 for j in range(MOE_EPG)], MOE_EPG)
    rest = [jnp.where(i1 == j, -jnp.inf, esel[j]) for j in range(MOE_EPG)]
    v2 = functools.reduce(jnp.maximum, rest)
    i2 = _first_index([(rest[j] == v2) & (i1 != j) for j in range(MOE_EPG)], MOE_EPG)
    e2 = jnp.exp(v2 - v1)
    w1 = gval / (1.0 + e2)
    w2 = gval * e2 / (1.0 + e2)
    a = jnp.minimum(i1, i2)
    b = jnp.maximum(i1, i2)
    wa = jnp.where(i1 < i2, w1, w2)
    wb = jnp.where(i1 < i2, w2, w1)
    cls = gidx * N_PAIRS + (a * (2 * MOE_EPG - 1 - a)) // 2 + (b - a - 1)
    cls_ref[0] = cls

    crow = lax.broadcasted_iota(jnp.int32, (CLS_ROWS, TM), 0)
    onehot = (crow == cls).astype(F32)
    prefix = jnp.dot(onehot.astype(BF16), tri_ref[...], preferred_element_type=F32)
    before = carry[:, 0:1]
    rank = jnp.sum(onehot * (prefix - 1.0 + before), axis=0, keepdims=True)
    rank_ref[0] = rank.astype(jnp.int32)
    carry[...] = carry[...] + jnp.sum(onehot, axis=1, keepdims=True)
    cnt_ref[...] = carry[...]

    wrow = lax.broadcasted_iota(jnp.int32, (128, TM), 0)
    wmat = jnp.where(wrow == 0, wa, jnp.where(wrow == 1, wb, 0.0))
    rows_ref[pl.ds(W_ROW, TM, stride=BLOCK_ROWS), :] = lax.bitcast_convert_type(wmat.T, jnp.int32)
    for k in range(W_ROW + 1, BLOCK_ROWS):
        rows_ref[pl.ds(k, TM, stride=BLOCK_ROWS), :] = jnp.zeros((TM, 128), jnp.int32)


def _router(x, g, wr, br, tri):
    n = x.shape[0]
    nt = n // TM
    return pl.pallas_call(
        _router_kernel,
        grid=(nt,),
        in_specs=[pl.BlockSpec((TM, D_MODEL), lambda i: (i, 0)), _resident((1, D_MODEL)),
                  _resident(wr.shape), _resident(br.shape), _resident(tri.shape)],
        out_specs=[pl.BlockSpec((TM * BLOCK_ROWS, 128), lambda i: (i, 0)),
                   pl.BlockSpec((1, 1, TM), lambda i: (i, 0, 0)),
                   pl.BlockSpec((1, 1, TM), lambda i: (i, 0, 0)),
                   pl.BlockSpec((CLS_ROWS, 128), lambda i: (0, 0))],
        out_shape=[jax.ShapeDtypeStruct((n * BLOCK_ROWS, 128), jnp.int32),
                   jax.ShapeDtypeStruct((nt, 1, TM), jnp.int32),
                   jax.ShapeDtypeStruct((nt, 1, TM), jnp.int32),
                   jax.ShapeDtypeStruct((CLS_ROWS, 128), F32)],
        scratch_shapes=[pltpu.VMEM((CLS_ROWS, 128), F32)],
        compiler_params=_cparams("arbitrary"),
        name="router",
    )(x, g, wr, br, tri)


def _block(ref, row):
    return ref.at[pl.ds(pl.multiple_of(row * BLOCK_ROWS, BLOCK_ROWS), BLOCK_ROWS)]


def _start_rows(copy):
    def group(g, c):
        for u in range(DMA_UNROLL):
            copy(g * DMA_UNROLL + u).start(priority=u % 2)
        return c

    lax.fori_loop(0, TM // DMA_UNROLL, group, 0)


def _wait_rows(copy):
    def wait(j, c):
        copy(j).wait()
        return c

    lax.fori_loop(0, TM, wait, 0, unroll=DMA_UNROLL)


def _scatter_kernel(dest_ref, rows_ref, hs_in_ref, hs_ref, sem):
    del hs_in_ref

    def copy(j):
        return pltpu.make_async_copy(_block(rows_ref, j), _block(hs_ref, dest_ref[0, 0, j]), sem)

    _start_rows(copy)
    _wait_rows(copy)


def _scatter(dest, rows, n_sorted):
    nt = dest.shape[0]
    hs0 = jnp.zeros((n_sorted * BLOCK_ROWS, 128), jnp.int32)
    return pl.pallas_call(
        _scatter_kernel,
        grid=(nt,),
        in_specs=[pl.BlockSpec((1, 1, TM), lambda i: (i, 0, 0), memory_space=pltpu.SMEM),
                  pl.BlockSpec((TM * BLOCK_ROWS, 128), lambda i: (i, 0)),
                  pl.BlockSpec(memory_space=pl.ANY)],
        out_specs=pl.BlockSpec(memory_space=pl.ANY),
        scratch_shapes=[pltpu.SemaphoreType.DMA(())],
        out_shape=jax.ShapeDtypeStruct((n_sorted * BLOCK_ROWS, 128), jnp.int32),
        input_output_aliases={2: 0},
        compiler_params=_cparams("arbitrary"),
        name="row_scatter",
    )(dest, rows, hs0)


def _gather_kernel(dest_ref, dest_next_ref, x_ref, ys_ref, o_ref, buf, sem):
    i = pl.program_id(0)
    slot = lax.rem(i, 2)

    def copy(d_ref, s, j):
        return pltpu.make_async_copy(_block(ys_ref, d_ref[0, 0, j]), _block(buf.at[s], j), sem.at[s])

    @pl.when(i == 0)
    def _():
        _start_rows(functools.partial(copy, dest_ref, slot))

    @pl.when(i + 1 < pl.num_programs(0))
    def _():
        _start_rows(functools.partial(copy, dest_next_ref, 1 - slot))

    _wait_rows(functools.partial(copy, dest_ref, slot))
    for k in range(D_MODEL // 128):
        lanes = slice(128 * k, 128 * (k + 1))
        o_ref[:, lanes] = x_ref[:, lanes] + buf[slot, pl.ds(k, TM, stride=BLOCK_ROWS), :]


def _gather_residual(dest, x, ys):
    n = x.shape[0]
    nt = n // TM
    return pl.pallas_call(
        _gather_kernel,
        grid=(nt,),
        in_specs=[pl.BlockSpec((1, 1, TM), lambda i: (i, 0, 0), memory_space=pltpu.SMEM),
                  pl.BlockSpec((1, 1, TM), lambda i: (jnp.minimum(i + 1, nt - 1), 0, 0), memory_space=pltpu.SMEM),
                  pl.BlockSpec((TM, D_MODEL), lambda i: (i, 0)),
                  pl.BlockSpec(memory_space=pl.ANY)],
        out_specs=pl.BlockSpec((TM, D_MODEL), lambda i: (i, 0)),
        scratch_shapes=[pltpu.VMEM((2, TM * BLOCK_ROWS, 128), F32), pltpu.SemaphoreType.DMA((2,))],
        out_shape=jax.ShapeDtypeStruct((n, D_MODEL), F32),
        compiler_params=_cparams("arbitrary"),
        name="row_gather",
    )(dest, dest, x, ys)


def _expert_kernel(e1_ref, e2_ref, valid_ref, hs_ref, wgu1_ref, wgu2_ref, wd1_ref, wd2_ref, ys_ref):
    i = pl.program_id(0)

    @pl.when(valid_ref[i] != 0)
    def _():
        words = [hs_ref[pl.ds(k, TME, stride=BLOCK_ROWS), :] for k in range(D_MODEL // 256)]
        lo = [lax.bitcast_convert_type(lax.shift_left(w, 16), F32).astype(BF16) for w in words]
        hi = [lax.bitcast_convert_type(w & HIGH_HALF, F32).astype(BF16) for w in words]
        h = jnp.concatenate(lo + hi, axis=1)
        wts = lax.bitcast_convert_type(hs_ref[pl.ds(W_ROW, TME, stride=BLOCK_ROWS), :], F32)

        def expert(wgu_ref, wd_ref):
            gu = jnp.dot(h, wgu_ref[0], preferred_element_type=F32)
            gate = gu[:, 0:D_EXPERT]
            act = gate * jax.nn.sigmoid(gate) * gu[:, D_EXPERT:]
            return jnp.dot(act.astype(BF16), wd_ref[0], preferred_element_type=F32)

        y = wts[:, 0:1] * expert(wgu1_ref, wd1_ref) + wts[:, 1:2] * expert(wgu2_ref, wd2_ref)
        for k in range(D_MODEL // 128):
            ys_ref[pl.ds(k, TME, stride=BLOCK_ROWS), :] = y[:, 128 * k:128 * (k + 1)]

    @pl.when(valid_ref[i] == 0)
    def _():
        ys_ref[...] = jnp.zeros_like(ys_ref)


def _experts(tile_e1, tile_e2, tile_valid, hs, wgu, wd):
    n_sorted = hs.shape[0] // BLOCK_ROWS
    nt = n_sorted // TME
    return pl.pallas_call(
        _expert_kernel,
        grid_spec=pltpu.PrefetchScalarGridSpec(
            num_scalar_prefetch=3,
            grid=(nt,),
            in_specs=[
                pl.BlockSpec((TME * BLOCK_ROWS, 128), lambda i, e1, e2, v: (i, 0)),
                pl.BlockSpec((1, D_MODEL, 2 * D_EXPERT), lambda i, e1, e2, v: (e1[i], 0, 0)),
                pl.BlockSpec((1, D_MODEL, 2 * D_EXPERT), lambda i, e1, e2, v: (e2[i], 0, 0)),
                pl.BlockSpec((1, D_EXPERT, D_MODEL), lambda i, e1, e2, v: (e1[i], 0, 0)),
                pl.BlockSpec((1, D_EXPERT, D_MODEL), lambda i, e1, e2, v: (e2[i], 0, 0)),
            ],
            out_specs=pl.BlockSpec((TME * BLOCK_ROWS, 128), lambda i, e1, e2, v: (i, 0)),
        ),
        out_shape=jax.ShapeDtypeStruct((n_sorted * BLOCK_ROWS, 128), F32),
        compiler_params=_cparams("arbitrary"),
        name="experts",
    )(tile_e1, tile_e2, tile_valid, hs, wgu, wgu, wd, wd)


_PAIR_A = np.array([a for a in range(MOE_EPG) for b in range(a + 1, MOE_EPG)], np.int32)
_PAIR_B = np.array([b for a in range(MOE_EPG) for b in range(a + 1, MOE_EPG)], np.int32)


def _moe(x, lp):
    n = x.shape[0]
    rows, cls, rank, counts = _router(x, lp["norm_ffn_g"], lp["wr"], lp["br"], lp["tri"])
    n_tiles = n // TME + N_CLASSES
    cnt = counts[:N_CLASSES, 0].astype(jnp.int32)
    tiles_per_class = (cnt + TME - 1) // TME
    tile_end = jnp.cumsum(tiles_per_class)
    seg_start = (tile_end - tiles_per_class) * TME
    tile_id = jnp.arange(n_tiles, dtype=jnp.int32)
    tile_cls = jnp.sum((tile_id[:, None] >= tile_end[None, :]).astype(jnp.int32), axis=1)
    tile_valid = (tile_cls < N_CLASSES).astype(jnp.int32)
    tile_cls = jnp.minimum(tile_cls, N_CLASSES - 1)
    class_id = jnp.arange(N_CLASSES, dtype=jnp.int32)
    class_np = np.arange(N_CLASSES)
    class_e1 = jnp.asarray((class_np // N_PAIRS) * MOE_EPG + _PAIR_A[class_np % N_PAIRS], jnp.int32)
    class_e2 = jnp.asarray((class_np // N_PAIRS) * MOE_EPG + _PAIR_B[class_np % N_PAIRS], jnp.int32)
    tile_hot = (tile_cls[:, None] == class_id[None, :]).astype(jnp.int32)
    tile_e1 = jnp.sum(tile_hot * class_e1[None, :], axis=1)
    tile_e2 = jnp.sum(tile_hot * class_e2[None, :], axis=1)
    dest = rank + jnp.sum(jnp.where(cls[..., None] == class_id, seg_start, 0), axis=-1)
    hs = _scatter(dest, rows, n_tiles * TME)
    ys = _experts(tile_e1, tile_e2, tile_valid, hs, lp["wgu"], lp["wd"])
    return _gather_residual(dest, x, ys)


def _prep_layer(l, norm_mix_g, w_in, b_in, q_norm_g, k_norm_g, na_rpb, sc_conv_w, cf_conv_w, cf_conv_b,
                cf_norm_g, cf_norm_b, w_branch, w_out, norm_ffn_g, router_group_w, router_group_b,
                router_expert_w, router_expert_b, expert_w_gate, expert_w_up, expert_w_down):
    head = np.arange(NA_WIDTH) // NA_HEAD_DIM
    bd = jnp.asarray((head[:, None] == head[None, :]).astype(np.float32) / NA_HEAD_DIM, BF16)
    wr = jnp.zeros((CLS_ROWS, D_MODEL), F32)
    wr = wr.at[0:MOE_GROUPS].set(router_group_w[l].T.astype(F32))
    wr = wr.at[MOE_GROUPS:MOE_GROUPS + MOE_EXPERTS].set(router_expert_w[l].T.astype(F32))
    br = jnp.zeros((CLS_ROWS,), F32)
    br = br.at[0:MOE_GROUPS].set(router_group_b[l].astype(F32))
    br = br.at[MOE_GROUPS:MOE_GROUPS + MOE_EXPERTS].set(router_expert_b[l].astype(F32))
    tri = jnp.asarray(np.triu(np.ones((TM, TM), np.float32)), BF16)
    return dict(
        norm_mix_g=norm_mix_g[l].reshape(1, D_MODEL).astype(F32),
        w_in=w_in[l].astype(BF16),
        b_in=b_in[l].reshape(1, IN_COLS).astype(F32),
        qg=(jnp.tile(q_norm_g[l].astype(F32), NA_HEADS) * (NA_HEAD_DIM ** -0.5)).reshape(1, NA_WIDTH),
        kg=jnp.tile(k_norm_g[l].astype(F32), NA_HEADS).reshape(1, NA_WIDTH),
        bd=bd,
        tbias=_bias_table(na_rpb[l]),
        scw=sc_conv_w[l].astype(F32),
        cfw=cf_conv_w[l].astype(F32),
        cfb=cf_conv_b[l].reshape(1, CF_WIDTH).astype(F32),
        cfg=cf_norm_g[l].reshape(1, CF_WIDTH).astype(F32),
        cfbeta=cf_norm_b[l].reshape(1, CF_WIDTH).astype(F32),
        wb=w_branch[l].astype(BF16),
        wo=w_out[l].astype(BF16),
        norm_ffn_g=norm_ffn_g[l].reshape(1, D_MODEL).astype(F32),
        wr=jnp.concatenate([wr.astype(BF16), (wr - wr.astype(BF16).astype(F32)).astype(BF16)], axis=0),
        br=jnp.broadcast_to(br[:, None], (CLS_ROWS, 128)),
        tri=tri,
        wgu=jnp.concatenate([expert_w_gate[l], expert_w_up[l]], axis=-1).astype(BF16),
        wd=expert_w_down[l].astype(BF16),
    )


def _trunk(x, layers):
    bn, seq_len, _ = x.shape
    assert seq_len % TM == 0 and seq_len % (RB * GRID_W) == 0
    xf = x.reshape(bn * seq_len, D_MODEL)
    for lp in layers:
        qkv, scu, gates = _inproj(xf, lp["norm_mix_g"], lp["w_in"], lp["b_in"], lp["qg"], lp["kg"], lp["bd"])
        ya = _attention(qkv, bn, seq_len, lp["tbias"])
        xf = _merge(xf, ya, scu, gates, seq_len, lp["scw"], lp["cfw"], lp["cfb"], lp["cfg"], lp["cfbeta"],
                    lp["wb"], lp["wo"])
        xf = _moe(xf, lp)
    return xf.reshape(bn, seq_len, D_MODEL)


def kernel(x_prompt, x_sample, norm_mix_g, w_in, b_in, q_norm_g, k_norm_g, na_rpb, sc_conv_w, cf_conv_w, cf_conv_b, cf_norm_g, cf_norm_b, w_branch, w_out, norm_ffn_g, router_group_w, router_group_b, router_expert_w, router_expert_b, expert_w_gate, expert_w_up, expert_w_down):
    params = (norm_mix_g, w_in, b_in, q_norm_g, k_norm_g, na_rpb, sc_conv_w, cf_conv_w, cf_conv_b, cf_norm_g,
              cf_norm_b, w_branch, w_out, norm_ffn_g, router_group_w, router_group_b, router_expert_w,
              router_expert_b, expert_w_gate, expert_w_up, expert_w_down)
    layers = [_prep_layer(l, *params) for l in range(norm_mix_g.shape[0])]
    return (_trunk(x_prompt, layers), _trunk(x_sample, layers))
```

```python
import functools

import numpy as np
import jax
import jax.numpy as jnp
from jax import lax
from jax.experimental import pallas as pl
from jax.experimental.pallas import tpu as pltpu

F32 = jnp.float32
BF16 = jnp.bfloat16

D_MODEL = 1024
GRID_W = 64
NA_HEADS = 8
NA_HEAD_DIM = 64
NA_WIDTH = NA_HEADS * NA_HEAD_DIM
NA_KH = 8
NA_KW = 16
SC_WIDTH = 512
SC_KERNEL = 3
CF_WIDTH = 512
CF_KERNEL = 31
IN_COLS = 3 * NA_WIDTH + 3 * SC_WIDTH + 2 * CF_WIDTH + 3 * D_MODEL
MOE_GROUPS = 4
MOE_EPG = 4
MOE_EXPERTS = MOE_GROUPS * MOE_EPG
D_EXPERT = 256
EPS = 1e-6

N_PAIRS = MOE_EPG * (MOE_EPG - 1) // 2
N_CLASSES = MOE_GROUPS * N_PAIRS
CLS_ROWS = 32
BLOCK_ROWS = 8
W_ROW = 4
HIGH_HALF = -65536
DMA_UNROLL = 8
MASK_VALUE = -1e30

TM = 512
RB = 8
ATTN_ROWS = 4
HALO = 16
CONV_CHUNK = 128
SUBLANES = 8
TME = 256
TME_BIG = 512
VMEM_LIMIT = 56 * 1024 * 1024


def _cparams(*sem):
    return pltpu.CompilerParams(dimension_semantics=sem, vmem_limit_bytes=VMEM_LIMIT)


def _resident(shape):
    nd = len(shape)
    return pl.BlockSpec(shape, lambda *_: (0,) * nd, pipeline_mode=pl.Buffered(1))


def _inproj_kernel(x_ref, g_ref, w_ref, b_ref, qg_ref, kg_ref, bd_ref, qkv_ref, scu_ref, gates_ref):
    x = x_ref[...]
    ms = jnp.mean(x * x, axis=-1, keepdims=True)
    h = (x * lax.rsqrt(ms + EPS) * g_ref[...]).astype(BF16)

    def zcols(c0, c1):
        return jnp.dot(h, w_ref[:, c0:c1], preferred_element_type=F32) + b_ref[:, c0:c1]

    def head_norm(z, gain_ref):
        hm = jnp.dot((z * z).astype(BF16), bd_ref[...], preferred_element_type=F32)
        return z * lax.rsqrt(hm + EPS) * gain_ref[...]

    w = NA_WIDTH
    qkv_ref[:, 0:w] = head_norm(zcols(0, w), qg_ref).astype(BF16)
    qkv_ref[:, w:2 * w] = head_norm(zcols(w, 2 * w), kg_ref).astype(BF16)
    qkv_ref[:, 2 * w:3 * w] = zcols(2 * w, 3 * w).astype(BF16)
    o = 3 * w
    sc_x = zcols(o, o + 512)
    sc_b = zcols(o + 512, o + 1024)
    sc_c = zcols(o + 1024, o + 1536)
    scu_ref[:, 0:512] = sc_c * sc_x
    scu_ref[:, 512:1024] = sc_b
    cf_a = zcols(o + 1536, o + 2048)
    cf_gate = zcols(o + 2048, o + 2560)
    scu_ref[:, 1024:1536] = cf_a * jax.nn.sigmoid(cf_gate)
    o = o + 2560
    for c in range(0, 3 * D_MODEL, 512):
        gates_ref[:, c:c + 512] = jax.nn.sigmoid(zcols(o + c, o + c + 512)).astype(BF16)


def _inproj(x, g, w, b, qg, kg, bd):
    n = x.shape[0]
    return pl.pallas_call(
        _inproj_kernel,
        grid=(n // TM,),
        in_specs=[
            pl.BlockSpec((TM, D_MODEL), lambda i: (i, 0)),
            _resident((1, D_MODEL)),
            _resident((D_MODEL, IN_COLS)),
            _resident((1, IN_COLS)),
            _resident((1, NA_WIDTH)),
            _resident((1, NA_WIDTH)),
            _resident((NA_WIDTH, NA_WIDTH)),
        ],
        out_specs=[
            pl.BlockSpec((TM, 3 * NA_WIDTH), lambda i: (i, 0)),
            pl.BlockSpec((TM, 1536), lambda i: (i, 0)),
            pl.BlockSpec((TM, 3 * D_MODEL), lambda i: (i, 0)),
        ],
        out_shape=[
            jax.ShapeDtypeStruct((n, 3 * NA_WIDTH), BF16),
            jax.ShapeDtypeStruct((n, 1536), F32),
            jax.ShapeDtypeStruct((n, 3 * D_MODEL), BF16),
        ],
        compiler_params=_cparams("parallel"),
        name="inproj",
    )(x, g, w, b, qg, kg, bd)


def _attn_kernel(rows, q_ref, kp_ref, kc_ref, kn_ref, vp_ref, vc_ref, vn_ref, t_ref, o_ref, kbuf, vbuf,
                 s_scr, p_scr):
    blk = RB * GRID_W
    kbuf[0:blk] = kp_ref[...]
    kbuf[blk:2 * blk] = kc_ref[...]
    kbuf[2 * blk:3 * blk] = kn_ref[...]
    vbuf[0:blk] = vp_ref[...]
    vbuf[blk:2 * blk] = vc_ref[...]
    vbuf[2 * blk:3 * blk] = vn_ref[...]
    r0 = pl.program_id(1) * RB
    lane = lax.broadcasted_iota(jnp.int32, (GRID_W, 128), 1)
    low = lane < NA_HEAD_DIM
    nkeys = NA_KH * GRID_W

    npair = NA_HEADS // 2

    def rows_body(ib, carry):
        geo = []
        for rr in range(ATTN_ROWS):
            i = ib * ATTN_ROWS + rr
            r = r0 + i
            rs = jnp.clip(r - NA_KH // 2, 0, rows - NA_KH)
            geo.append((r - rs, pl.multiple_of((rs - r0 + RB) * GRID_W, GRID_W), pl.multiple_of(i * GRID_W, GRID_W)))
        for rr, (d, start, qrow) in enumerate(geo):
            for j in range(npair):
                cols = slice(128 * j, 128 * (j + 1))
                qp = q_ref[pl.ds(qrow, GRID_W), cols]
                zero = jnp.zeros_like(qp)
                q2 = jnp.concatenate([jnp.where(low, qp, zero), jnp.where(low, zero, qp)], axis=0)
                kw = kbuf[pl.ds(start, nkeys), cols]
                s = lax.dot_general(q2, kw, (((1,), (1,)), ((), ())), preferred_element_type=F32)
                s_scr[rr * npair + j] = s + t_ref[j, d]
        for u in range(ATTN_ROWS * npair):
            s = s_scr[u]
            p = jnp.exp(s - jnp.max(s, axis=-1, keepdims=True))
            inv = 1.0 / jnp.sum(p, axis=-1, keepdims=True)
            p_scr[u] = (p * inv).astype(BF16)
        for rr, (d, start, qrow) in enumerate(geo):
            for j in range(npair):
                cols = slice(128 * j, 128 * (j + 1))
                vw = vbuf[pl.ds(start, nkeys), cols]
                o2 = jnp.dot(p_scr[rr * npair + j], vw, preferred_element_type=F32)
                o_ref[pl.ds(qrow, GRID_W), cols] = jnp.where(low, o2[0:GRID_W], o2[GRID_W:]).astype(BF16)
        return carry

    lax.fori_loop(0, RB // ATTN_ROWS, rows_body, 0)


def _attention(qkv, bn, seq_len, tbias):
    rows = seq_len // GRID_W
    assert rows >= NA_KH and rows % RB == 0 and RB >= NA_KH // 2
    nrb = rows // RB
    blk = RB * GRID_W
    n = bn * seq_len

    def at(col, shift):
        def imap(b, i):
            return (b * nrb + jnp.clip(i + shift, 0, nrb - 1), col)
        return pl.BlockSpec((blk, NA_WIDTH), imap)

    return pl.pallas_call(
        functools.partial(_attn_kernel, rows),
        grid=(bn, nrb),
        in_specs=[at(0, 0), at(1, -1), at(1, 0), at(1, 1), at(2, -1), at(2, 0), at(2, 1),
                  _resident(tbias.shape)],
        out_specs=pl.BlockSpec((blk, NA_WIDTH), lambda b, i: (b * nrb + i, 0)),
        out_shape=jax.ShapeDtypeStruct((n, NA_WIDTH), BF16),
        scratch_shapes=[pltpu.VMEM((3 * blk, NA_WIDTH), BF16), pltpu.VMEM((3 * blk, NA_WIDTH), BF16),
                        pltpu.VMEM((ATTN_ROWS * NA_HEADS // 2, 2 * GRID_W, NA_KH * GRID_W), F32),
                        pltpu.VMEM((ATTN_ROWS * NA_HEADS // 2, 2 * GRID_W, NA_KH * GRID_W), BF16)],
        compiler_params=_cparams("parallel", "parallel"),
        name="nattn",
    )(qkv, qkv, qkv, qkv, qkv, qkv, qkv, tbias)


def _bias_table(rpb):
    cols = np.arange(GRID_W)
    cs = np.clip(cols - NA_KW // 2, 0, GRID_W - NA_KW)
    kc = np.arange(GRID_W)
    inside = (kc[None, :] >= cs[:, None]) & (kc[None, :] < cs[:, None] + NA_KW)
    co = kc[None, :] - cols[:, None] + NA_KW - 1
    sel = (co[:, :, None] == np.arange(2 * NA_KW - 1)[None, None, :]) & inside[:, :, None]
    t = jnp.einsum("hrx,ckx->hrck", rpb.astype(F32), jnp.asarray(sel, F32), precision=lax.Precision.HIGHEST)
    t = jnp.where(jnp.asarray(inside)[None, None], t, MASK_VALUE)
    t = jnp.stack([t[:, NA_KH - 1 - d:2 * NA_KH - 1 - d] for d in range(NA_KH)], axis=1)
    t = jnp.transpose(t, (0, 1, 3, 2, 4)).reshape(NA_HEADS // 2, 2, NA_KH, GRID_W, NA_KH * GRID_W)
    return jnp.transpose(t, (0, 2, 1, 3, 4)).reshape(NA_HEADS // 2, NA_KH, 2 * GRID_W, NA_KH * GRID_W)


def _conv_tile(t, nt_seq, scu_ref, hp_ref, hn_ref, scw_ref, cfw_ref, cfb_ref, cfg_ref, cfbeta_ref,
               ext_s, ext_u, cacc, ybc_ref):
    keep_p = t > 0
    keep_n = t < nt_seq - 1
    ext_s[0:HALO] = jnp.where(keep_p, hp_ref[:, 0:512].astype(F32), 0.0)
    ext_s[HALO:HALO + TM] = scu_ref[:, 0:512].astype(F32)
    ext_s[HALO + TM:] = jnp.where(keep_n, hn_ref[:, 0:512].astype(F32), 0.0)
    ext_u[0:HALO] = jnp.where(keep_p, hp_ref[:, 1024:1536].astype(F32), 0.0)
    ext_u[HALO:HALO + TM] = scu_ref[:, 1024:1536].astype(F32)
    ext_u[HALO + TM:] = jnp.where(keep_n, hn_ref[:, 1024:1536].astype(F32), 0.0)

    ch = CONV_CHUNK
    pad = CF_KERNEL // 2
    for cb in range(CF_WIDTH // 128):
        lanes = slice(128 * cb, 128 * (cb + 1))
        for ci in range(TM // ch):
            base = ci * ch
            acc = None
            for s in range(SUBLANES):
                ps = None
                for a in range((CF_KERNEL - s + SUBLANES - 1) // SUBLANES):
                    k = SUBLANES * a + s
                    lo = base + SUBLANES * a
                    term = ext_u[lo:lo + ch + SUBLANES, lanes] * cfw_ref[k:k + 1, lanes]
                    ps = term if ps is None else ps + term
                off = HALO - pad + s
                piece = ps[off:off + ch]
                acc = piece if acc is None else acc + piece
            cacc[base:base + ch, lanes] = acc
            win = ext_s[base + HALO - SUBLANES:base + HALO + ch + SUBLANES, lanes]
            accb = None
            for k in range(SC_KERNEL):
                off = SUBLANES - SC_KERNEL // 2 + k
                term = win[off:off + ch] * scw_ref[k:k + 1, lanes]
                accb = term if accb is None else accb + term
            scb = scu_ref[base:base + ch, 512 + 128 * cb:512 + 128 * (cb + 1)].astype(F32)
            ybc_ref[base:base + ch, lanes] = (scb * accb).astype(BF16)

    for ci in range(TM // ch):
        rows = slice(ci * ch, (ci + 1) * ch)
        acc = cacc[rows, :] + cfb_ref[...]
        mu = jnp.mean(acc, axis=-1, keepdims=True)
        xc = acc - mu
        var = jnp.mean(xc * xc, axis=-1, keepdims=True)
        y = xc * lax.rsqrt(var + EPS) * cfg_ref[...] + cfbeta_ref[...]
        ybc_ref[rows, SC_WIDTH:] = (y * jax.nn.sigmoid(y)).astype(BF16)


def _merge_kernel(nt_seq, x_ref, ya_ref, gates_ref, scu_ref, hp_ref, hn_ref, scw_ref, cfw_ref, cfb_ref, cfg_ref,
                  cfbeta_ref, wb_ref, wo_ref, o_ref, ext_s, ext_u, cacc, ybc):
    _conv_tile(lax.rem(pl.program_id(0), nt_seq), nt_seq, scu_ref, hp_ref, hn_ref, scw_ref, cfw_ref, cfb_ref,
               cfg_ref, cfbeta_ref, ext_s, ext_u, cacc, ybc)
    def branch(b, y):
        proj = jnp.dot(y, wb_ref[b], preferred_element_type=F32).astype(BF16)
        return gates_ref[:, b * D_MODEL:(b + 1) * D_MODEL] * proj

    merged = branch(0, ya_ref[...]) + branch(1, ybc[:, 0:SC_WIDTH]) + branch(2, ybc[:, SC_WIDTH:])
    o_ref[...] = x_ref[...] + jnp.dot(merged, wo_ref[...], preferred_element_type=F32)


def _merge(x, ya, scu, gates, seq_len, scw, cfw, cfb, cfg, cfbeta, wb, wo):
    n = x.shape[0]
    hb = TM // HALO

    def tile(width):
        return pl.BlockSpec((TM, width), lambda i: (i, 0))

    halo_prev = pl.BlockSpec((HALO, 1536), lambda i: (jnp.maximum(i * hb - 1, 0), 0))
    halo_next = pl.BlockSpec((HALO, 1536), lambda i: (jnp.minimum((i + 1) * hb, n // HALO - 1), 0))
    return pl.pallas_call(
        functools.partial(_merge_kernel, seq_len // TM),
        grid=(n // TM,),
        in_specs=[tile(D_MODEL), tile(NA_WIDTH), tile(3 * D_MODEL), tile(1536), halo_prev, halo_next,
                  _resident(scw.shape), _resident(cfw.shape), _resident(cfb.shape), _resident(cfg.shape),
                  _resident(cfbeta.shape), _resident(wb.shape), _resident(wo.shape)],
        out_specs=tile(D_MODEL),
        out_shape=jax.ShapeDtypeStruct((n, D_MODEL), F32),
        scratch_shapes=[pltpu.VMEM((TM + 2 * HALO, SC_WIDTH), F32), pltpu.VMEM((TM + 2 * HALO, CF_WIDTH), F32),
                        pltpu.VMEM((TM, CF_WIDTH), F32), pltpu.VMEM((TM, SC_WIDTH + CF_WIDTH), BF16)],
        compiler_params=_cparams("parallel"),
        name="merge",
    )(x, ya, gates, scu, scu, scu, scw, cfw, cfb, cfg, cfbeta, wb, wo)


def _first_index(hit, n):
    idx = jnp.full(hit[0].shape, n, jnp.int32)
    for i in range(n - 1, -1, -1):
        idx = jnp.where(hit[i], i, idx)
    return idx


def _router_kernel(x_ref, g_ref, wr_ref, br_ref, tri_ref, rows_ref, cls_ref, rank_ref, cnt_ref, carry):
    @pl.when(pl.program_id(0) == 0)
    def _():
        carry[...] = jnp.zeros_like(carry)

    x = x_ref[...]
    ms = jnp.mean(x * x, axis=-1, keepdims=True)
    h = x * lax.rsqrt(ms + EPS) * g_ref[...]
    h_hi = h.astype(BF16)
    h_hi32 = h_hi.astype(F32)
    bits = lax.bitcast_convert_type(h_hi32, jnp.int32)
    half = D_MODEL // 2
    for k in range(half // 128):
        lo = lax.shift_right_logical(bits[:, 128 * k:128 * (k + 1)], 16)
        hi = bits[:, half + 128 * k:half + 128 * (k + 1)] & HIGH_HALF
        rows_ref[pl.ds(k, TM, stride=BLOCK_ROWS), :] = hi | lo
    h_lo = (h - h_hi32).astype(BF16)
    nt_dims = (((1,), (1,)), ((), ()))
    part = lax.dot_general(wr_ref[...], h_hi, nt_dims, preferred_element_type=F32)
    logits = (part[0:CLS_ROWS] + part[CLS_ROWS:]
              + lax.dot_general(wr_ref[0:CLS_ROWS, :], h_lo, nt_dims, preferred_element_type=F32)
              + br_ref[:, 0:1])
    gl = [logits[g:g + 1, :] for g in range(MOE_GROUPS)]
    gmax = functools.reduce(jnp.maximum, gl)
    gidx = _first_index([gl[g] == gmax for g in range(MOE_GROUPS)], MOE_GROUPS)
    gval = 1.0 / functools.reduce(lambda a, b: a + b, [jnp.exp(v - gmax) for v in gl])
    esel = []
    for j in range(MOE_EPG):
        v = jnp.zeros_like(gmax)
        for g in range(MOE_GROUPS):
            row = MOE_GROUPS + g * MOE_EPG + j
            v = jnp.where(gidx == g, logits[row:row + 1, :], v)
        esel.append(v)
    v1 = functools.reduce(jnp.maximum, esel)
    i1 = _first_index([esel[j] == v1 for j in range(MOE_EPG)], MOE_EPG)
    rest = [jnp.where(i1 == j, -jnp.inf, esel[j]) for j in range(MOE_EPG)]
    v2 = functools.reduce(jnp.maximum, rest)
    i2 = _first_index([(rest[j] == v2) & (i1 != j) for j in range(MOE_EPG)], MOE_EPG)
    e2 = jnp.exp(v2 - v1)
    w1 = gval / (1.0 + e2)
    w2 = gval * e2 / (1.0 + e2)
    a = jnp.minimum(i1, i2)
    b = jnp.maximum(i1, i2)
    wa = jnp.where(i1 < i2, w1, w2)
    wb = jnp.where(i1 < i2, w2, w1)
    cls = gidx * N_PAIRS + (a * (2 * MOE_EPG - 1 - a)) // 2 + (b - a - 1)
    cls_ref[0] = cls

    crow = lax.broadcasted_iota(jnp.int32, (CLS_ROWS, TM), 0)
    onehot = (crow == cls).astype(F32)
    prefix = jnp.dot(onehot.astype(BF16), tri_ref[...], preferred_element_type=F32)
    before = carry[:, 0:1]
    rank = jnp.sum(onehot * (prefix - 1.0 + before), axis=0, keepdims=True)
    rank_ref[0] = rank.astype(jnp.int32)
    carry[...] = carry[...] + jnp.sum(onehot, axis=1, keepdims=True)
    cnt_ref[...] = carry[...]

    wrow = lax.broadcasted_iota(jnp.int32, (128, TM), 0)
    wmat = jnp.where(wrow == 0, wa, jnp.where(wrow == 1, wb, 0.0))
    rows_ref[pl.ds(W_ROW, TM, stride=BLOCK_ROWS), :] = lax.bitcast_convert_type(wmat.T, jnp.int32)
    for k in range(W_ROW + 1, BLOCK_ROWS):
        rows_ref[pl.ds(k, TM, stride=BLOCK_ROWS), :] = jnp.zeros((TM, 128), jnp.int32)


def _router(x, g, wr, br, tri):
    n = x.shape[0]
    nt = n // TM
    return pl.pallas_call(
        _router_kernel,
        grid=(nt,),
        in_specs=[pl.BlockSpec((TM, D_MODEL), lambda i: (i, 0)), _resident((1, D_MODEL)),
                  _resident(wr.shape), _resident(br.shape), _resident(tri.shape)],
        out_specs=[pl.BlockSpec((TM * BLOCK_ROWS, 128), lambda i: (i, 0)),
                   pl.BlockSpec((1, 1, TM), lambda i: (i, 0, 0)),
                   pl.BlockSpec((1, 1, TM), lambda i: (i, 0, 0)),
                   pl.BlockSpec((CLS_ROWS, 128), lambda i: (0, 0))],
        out_shape=[jax.ShapeDtypeStruct((n * BLOCK_ROWS, 128), jnp.int32),
                   jax.ShapeDtypeStruct((nt, 1, TM), jnp.int32),
                   jax.ShapeDtypeStruct((nt, 1, TM), jnp.int32),
                   jax.ShapeDtypeStruct((CLS_ROWS, 128), F32)],
        scratch_shapes=[pltpu.VMEM((CLS_ROWS, 128), F32)],
        compiler_params=_cparams("arbitrary"),
        name="router",
    )(x, g, wr, br, tri)


def _block(ref, row):
    return ref.at[pl.ds(pl.multiple_of(row * BLOCK_ROWS, BLOCK_ROWS), BLOCK_ROWS)]


def _start_rows(copy):
    def group(g, c):
        for u in range(DMA_UNROLL):
            copy(g * DMA_UNROLL + u).start(priority=u % 2)
        return c

    lax.fori_loop(0, TM // DMA_UNROLL, group, 0)


def _wait_rows(copy):
    def wait(j, c):
        copy(j).wait()
        return c

    lax.fori_loop(0, TM, wait, 0, unroll=DMA_UNROLL)


def _scatter_kernel(tme, ztile_ref, dest_ref, rows_ref, hs_ref, zeros, sem, zsem):
    @pl.when(pl.program_id(0) == 0)
    def _():
        zeros[...] = jnp.zeros_like(zeros)

        def zcopy(c):
            start = pl.multiple_of(ztile_ref[c] * (tme * BLOCK_ROWS), BLOCK_ROWS)
            return pltpu.make_async_copy(zeros, hs_ref.at[pl.ds(start, tme * BLOCK_ROWS)], zsem)

        for c in range(2 * N_CLASSES):
            @pl.when(ztile_ref[c] >= 0)
            def _():
                zcopy(c).start()
        for c in range(2 * N_CLASSES):
            @pl.when(ztile_ref[c] >= 0)
            def _():
                zcopy(c).wait()

    def copy(j):
        return pltpu.make_async_copy(_block(rows_ref, j), _block(hs_ref, dest_ref[0, 0, j]), sem)

    _start_rows(copy)
    _wait_rows(copy)


def _scatter(zero_tiles, dest, rows, n_sorted, tme):
    nt = dest.shape[0]
    return pl.pallas_call(
        functools.partial(_scatter_kernel, tme),
        grid_spec=pltpu.PrefetchScalarGridSpec(
            num_scalar_prefetch=1,
            grid=(nt,),
            in_specs=[pl.BlockSpec((1, 1, TM), lambda i, z: (i, 0, 0), memory_space=pltpu.SMEM),
                      pl.BlockSpec((TM * BLOCK_ROWS, 128), lambda i, z: (i, 0))],
            out_specs=pl.BlockSpec(memory_space=pl.ANY),
            scratch_shapes=[pltpu.VMEM((tme * BLOCK_ROWS, 128), jnp.int32), pltpu.SemaphoreType.DMA(()),
                            pltpu.SemaphoreType.DMA(())],
        ),
        out_shape=jax.ShapeDtypeStruct((n_sorted * BLOCK_ROWS, 128), jnp.int32),
        compiler_params=_cparams("arbitrary"),
        name="row_scatter",
    )(zero_tiles, dest, rows)


def _gather_kernel(dest_ref, dest_next_ref, x_ref, ys_ref, o_ref, buf, sem):
    i = pl.program_id(0)
    slot = lax.rem(i, 2)

    def copy(d_ref, s, j):
        return pltpu.make_async_copy(_block(ys_ref, d_ref[0, 0, j]), _block(buf.at[s], j), sem.at[s])

    @pl.when(i == 0)
    def _():
        _start_rows(functools.partial(copy, dest_ref, slot))

    @pl.when(i + 1 < pl.num_programs(0))
    def _():
        _start_rows(functools.partial(copy, dest_next_ref, 1 - slot))

    _wait_rows(functools.partial(copy, dest_ref, slot))
    for k in range(D_MODEL // 128):
        lanes = slice(128 * k, 128 * (k + 1))
        o_ref[:, lanes] = x_ref[:, lanes] + buf[slot, pl.ds(k, TM, stride=BLOCK_ROWS), :]


def _gather_residual(dest, x, ys):
    n = x.shape[0]
    nt = n // TM
    return pl.pallas_call(
        _gather_kernel,
        grid=(nt,),
        in_specs=[pl.BlockSpec((1, 1, TM), lambda i: (i, 0, 0), memory_space=pltpu.SMEM),
                  pl.BlockSpec((1, 1, TM), lambda i: (jnp.minimum(i + 1, nt - 1), 0, 0), memory_space=pltpu.SMEM),
                  pl.BlockSpec((TM, D_MODEL), lambda i: (i, 0)),
                  pl.BlockSpec(memory_space=pl.ANY)],
        out_specs=pl.BlockSpec((TM, D_MODEL), lambda i: (i, 0)),
        scratch_shapes=[pltpu.VMEM((2, TM * BLOCK_ROWS, 128), F32), pltpu.SemaphoreType.DMA((2,))],
        out_shape=jax.ShapeDtypeStruct((n, D_MODEL), F32),
        compiler_params=_cparams("arbitrary"),
        name="row_gather",
    )(dest, dest, x, ys)


def _expert_kernel(tme, e1_ref, e2_ref, src_ref, hs_ref, wgu1_ref, wgu2_ref, wd1_ref, wd2_ref, ys_ref):
    i = pl.program_id(0)

    @pl.when(src_ref[i] == i)
    def _():
        words = [hs_ref[pl.ds(k, tme, stride=BLOCK_ROWS), :] for k in range(D_MODEL // 256)]
        lo = [lax.bitcast_convert_type(lax.shift_left(w, 16), F32).astype(BF16) for w in words]
        hi = [lax.bitcast_convert_type(w & HIGH_HALF, F32).astype(BF16) for w in words]
        h = jnp.concatenate(lo + hi, axis=1)
        wts = lax.bitcast_convert_type(hs_ref[pl.ds(W_ROW, tme, stride=BLOCK_ROWS), :], F32)

        def expert(wgu_ref, wd_ref):
            gu = jnp.dot(h, wgu_ref[0], preferred_element_type=F32)
            gate = gu[:, 0:D_EXPERT]
            act = gate * jax.nn.sigmoid(gate) * gu[:, D_EXPERT:]
            return jnp.dot(act.astype(BF16), wd_ref[0], preferred_element_type=F32)

        y = wts[:, 0:1] * expert(wgu1_ref, wd1_ref) + wts[:, 1:2] * expert(wgu2_ref, wd2_ref)
        for k in range(D_MODEL // 128):
            ys_ref[pl.ds(k, tme, stride=BLOCK_ROWS), :] = y[:, 128 * k:128 * (k + 1)]

    @pl.when(src_ref[i] != i)
    def _():
        ys_ref[...] = jnp.zeros_like(ys_ref)


def _experts(tile_e1, tile_e2, tile_src, hs, wgu, wd, tme):
    n_sorted = hs.shape[0] // BLOCK_ROWS
    nt = n_sorted // tme
    return pl.pallas_call(
        functools.partial(_expert_kernel, tme),
        grid_spec=pltpu.PrefetchScalarGridSpec(
            num_scalar_prefetch=3,
            grid=(nt,),
            in_specs=[
                pl.BlockSpec((tme * BLOCK_ROWS, 128), lambda i, e1, e2, src: (src[i], 0)),
                pl.BlockSpec((1, D_MODEL, 2 * D_EXPERT), lambda i, e1, e2, src: (e1[i], 0, 0)),
                pl.BlockSpec((1, D_MODEL, 2 * D_EXPERT), lambda i, e1, e2, src: (e2[i], 0, 0)),
                pl.BlockSpec((1, D_EXPERT, D_MODEL), lambda i, e1, e2, src: (e1[i], 0, 0)),
                pl.BlockSpec((1, D_EXPERT, D_MODEL), lambda i, e1, e2, src: (e2[i], 0, 0)),
            ],
            out_specs=pl.BlockSpec((tme * BLOCK_ROWS, 128), lambda i, e1, e2, src: (i, 0)),
        ),
        out_shape=jax.ShapeDtypeStruct((n_sorted * BLOCK_ROWS, 128), F32),
        compiler_params=_cparams("arbitrary"),
        name="experts",
    )(tile_e1, tile_e2, tile_src, hs, wgu, wgu, wd, wd)


_PAIR_A = np.array([a for a in range(MOE_EPG) for b in range(a + 1, MOE_EPG)], np.int32)
_PAIR_B = np.array([b for a in range(MOE_EPG) for b in range(a + 1, MOE_EPG)], np.int32)


def _moe(x, lp):
    n = x.shape[0]
    rows, cls, rank, counts = _router(x, lp["norm_ffn_g"], lp["wr"], lp["br"], lp["tri"])
    tme = TME_BIG if n >= 4 * N_CLASSES * TME_BIG else TME
    n_tiles = n // tme + N_CLASSES
    cnt = counts[:N_CLASSES, 0].astype(jnp.int32)
    tiles_per_class = (cnt + tme - 1) // tme
    tile_end = jnp.cumsum(tiles_per_class)
    seg_start = (tile_end - tiles_per_class) * tme
    last_tile = jnp.where(tiles_per_class > 0, tile_end - 1, -1)
    tile_id = jnp.arange(n_tiles, dtype=jnp.int32)
    tile_cls = jnp.sum((tile_id[:, None] >= tile_end[None, :]).astype(jnp.int32), axis=1)
    tile_src = jnp.minimum(tile_id, tile_end[-1] - 1)
    tile_cls = jnp.minimum(tile_cls, N_CLASSES - 1)
    class_id = jnp.arange(N_CLASSES, dtype=jnp.int32)
    class_np = np.arange(N_CLASSES)
    class_e1 = jnp.asarray((class_np // N_PAIRS) * MOE_EPG + _PAIR_A[class_np % N_PAIRS], jnp.int32)
    class_e2 = jnp.asarray((class_np // N_PAIRS) * MOE_EPG + _PAIR_B[class_np % N_PAIRS], jnp.int32)
    tile_hot = (tile_cls[:, None] == class_id[None, :]).astype(jnp.int32)
    tile_e1 = jnp.sum(tile_hot * class_e1[None, :], axis=1)
    tile_e2 = jnp.sum(tile_hot * class_e2[None, :], axis=1)
    dest = rank + jnp.sum(jnp.where(cls[..., None] == class_id, seg_start, 0), axis=-1)
    spare = tile_end[-1] + jnp.arange(N_CLASSES, dtype=jnp.int32)
    zero_tiles = jnp.concatenate([last_tile, jnp.where(spare < n_tiles, spare, -1)]).astype(jnp.int32)
    hs = _scatter(zero_tiles, dest, rows, n_tiles * tme, tme)
    ys = _experts(tile_e1, tile_e2, tile_src, hs, lp["wgu"], lp["wd"], tme)
    return _gather_residual(dest, x, ys)


def _prep_layer(l, norm_mix_g, w_in, b_in, q_norm_g, k_norm_g, na_rpb, sc_conv_w, cf_conv_w, cf_conv_b,
                cf_norm_g, cf_norm_b, w_branch, w_out, norm_ffn_g, router_group_w, router_group_b,
                router_expert_w, router_expert_b, expert_w_gate, expert_w_up, expert_w_down):
    head = np.arange(NA_WIDTH) // NA_HEAD_DIM
    bd = jnp.asarray((head[:, None] == head[None, :]).astype(np.float32) / NA_HEAD_DIM, BF16)
    wr = jnp.zeros((CLS_ROWS, D_MODEL), F32)
    wr = wr.at[0:MOE_GROUPS].set(router_group_w[l].T.astype(F32))
    wr = wr.at[MOE_GROUPS:MOE_GROUPS + MOE_EXPERTS].set(router_expert_w[l].T.astype(F32))
    br = jnp.zeros((CLS_ROWS,), F32)
    br = br.at[0:MOE_GROUPS].set(router_group_b[l].astype(F32))
    br = br.at[MOE_GROUPS:MOE_GROUPS + MOE_EXPERTS].set(router_expert_b[l].astype(F32))
    tri = jnp.asarray(np.triu(np.ones((TM, TM), np.float32)), BF16)
    return dict(
        norm_mix_g=norm_mix_g[l].reshape(1, D_MODEL).astype(F32),
        w_in=w_in[l].astype(BF16),
        b_in=b_in[l].reshape(1, IN_COLS).astype(F32),
        qg=(jnp.tile(q_norm_g[l].astype(F32), NA_HEADS) * (NA_HEAD_DIM ** -0.5)).reshape(1, NA_WIDTH),
        kg=jnp.tile(k_norm_g[l].astype(F32), NA_HEADS).reshape(1, NA_WIDTH),
        bd=bd,
        tbias=_bias_table(na_rpb[l]),
        scw=sc_conv_w[l].astype(F32),
        cfw=cf_conv_w[l].astype(F32),
        cfb=cf_conv_b[l].reshape(1, CF_WIDTH).astype(F32),
        cfg=cf_norm_g[l].reshape(1, CF_WIDTH).astype(F32),
        cfbeta=cf_norm_b[l].reshape(1, CF_WIDTH).astype(F32),
        wb=w_branch[l].astype(BF16),
        wo=w_out[l].astype(BF16),
        norm_ffn_g=norm_ffn_g[l].reshape(1, D_MODEL).astype(F32),
        wr=jnp.concatenate([wr.astype(BF16), (wr - wr.astype(BF16).astype(F32)).astype(BF16)], axis=0),
        br=jnp.broadcast_to(br[:, None], (CLS_ROWS, 128)),
        tri=tri,
        wgu=jnp.concatenate([expert_w_gate[l], expert_w_up[l]], axis=-1).astype(BF16),
        wd=expert_w_down[l].astype(BF16),
    )


def _trunk(x, layers):
    bn, seq_len, _ = x.shape
    assert seq_len % TM == 0 and seq_len % (RB * GRID_W) == 0
    xf = x.reshape(bn * seq_len, D_MODEL)
    for lp in layers:
        qkv, scu, gates = _inproj(xf, lp["norm_mix_g"], lp["w_in"], lp["b_in"], lp["qg"], lp["kg"], lp["bd"])
        ya = _attention(qkv, bn, seq_len, lp["tbias"])
        xf = _merge(xf, ya, scu, gates, seq_len, lp["scw"], lp["cfw"], lp["cfb"], lp["cfg"], lp["cfbeta"],
                    lp["wb"], lp["wo"])
        xf = _moe(xf, lp)
    return xf.reshape(bn, seq_len, D_MODEL)


def kernel(x_prompt, x_sample, norm_mix_g, w_in, b_in, q_norm_g, k_norm_g, na_rpb, sc_conv_w, cf_conv_w, cf_conv_b, cf_norm_g, cf_norm_b, w_branch, w_out, norm_ffn_g, router_group_w, router_group_b, router_expert_w, router_expert_b, expert_w_gate, expert_w_up, expert_w_down):
    params = (norm_mix_g, w_in, b_in, q_norm_g, k_norm_g, na_rpb, sc_conv_w, cf_conv_w, cf_conv_b, cf_norm_g,
              cf_norm_b, w_branch, w_out, norm_ffn_g, router_group_w, router_group_b, router_expert_w,
              router_expert_b, expert_w_gate, expert_w_up, expert_w_down)
    layers = [_prep_layer(l, *params) for l in range(norm_mix_g.shape[0])]
    return (_trunk(x_prompt, layers), _trunk(x_sample, layers))
```

```python
import functools

import numpy as np
import jax
import jax.numpy as jnp
from jax import lax
from jax.experimental import pallas as pl
from jax.experimental.pallas import tpu as pltpu

F32 = jnp.float32
BF16 = jnp.bfloat16

D_MODEL = 1024
GRID_W = 64
NA_HEADS = 8
NA_HEAD_DIM = 64
NA_WIDTH = NA_HEADS * NA_HEAD_DIM
NA_KH = 8
NA_KW = 16
SC_WIDTH = 512
SC_KERNEL = 3
CF_WIDTH = 512
CF_KERNEL = 31
IN_COLS = 3 * NA_WIDTH + 3 * SC_WIDTH + 2 * CF_WIDTH + 3 * D_MODEL
MOE_GROUPS = 4
MOE_EPG = 4
MOE_EXPERTS = MOE_GROUPS * MOE_EPG
D_EXPERT = 256
EPS = 1e-6

N_PAIRS = MOE_EPG * (MOE_EPG - 1) // 2
N_CLASSES = MOE_GROUPS * N_PAIRS
CLS_ROWS = 32
BLOCK_ROWS = 8
W_ROW = 4
HIGH_HALF = -65536
DMA_UNROLL = 8
MASK_VALUE = -1e30

TM = 512
RB = 8
ATTN_ROWS = 4
HALO = 16
CONV_CHUNK = 128
SUBLANES = 8
TME = 256
TME_BIG = 512
VMEM_LIMIT = 56 * 1024 * 1024


def _cparams(*sem):
    return pltpu.CompilerParams(dimension_semantics=sem, vmem_limit_bytes=VMEM_LIMIT)


def _resident(shape):
    nd = len(shape)
    return pl.BlockSpec(shape, lambda *_: (0,) * nd, pipeline_mode=pl.Buffered(1))


def _inproj_kernel(x_ref, g_ref, w_ref, b_ref, qg_ref, kg_ref, bd_ref, qkv_ref, scu_ref, gates_ref):
    _inproj_body(x_ref[...], g_ref, w_ref, b_ref, qg_ref, kg_ref, bd_ref, qkv_ref, scu_ref, gates_ref)


def _inproj_gather_kernel(dest_ref, dest_next_ref, x_ref, ys_ref, g_ref, w_ref, b_ref, qg_ref, kg_ref, bd_ref,
                          xo_ref, qkv_ref, scu_ref, gates_ref, buf, sem):
    slot = _fetch_token_blocks(dest_ref, dest_next_ref, ys_ref, buf, sem)
    for k in range(D_MODEL // 128):
        lanes = slice(128 * k, 128 * (k + 1))
        xo_ref[:, lanes] = x_ref[:, lanes] + buf[slot, pl.ds(k, TM, stride=BLOCK_ROWS), :]
    _inproj_body(xo_ref[...], g_ref, w_ref, b_ref, qg_ref, kg_ref, bd_ref, qkv_ref, scu_ref, gates_ref)


def _inproj_body(x, g_ref, w_ref, b_ref, qg_ref, kg_ref, bd_ref, qkv_ref, scu_ref, gates_ref):
    ms = jnp.mean(x * x, axis=-1, keepdims=True)
    h = (x * lax.rsqrt(ms + EPS) * g_ref[...]).astype(BF16)

    def zcols(c0, c1):
        return jnp.dot(h, w_ref[:, c0:c1], preferred_element_type=F32) + b_ref[:, c0:c1]

    def head_norm(z, gain_ref):
        hm = jnp.dot((z * z).astype(BF16), bd_ref[...], preferred_element_type=F32)
        return z * lax.rsqrt(hm + EPS) * gain_ref[...]

    w = NA_WIDTH
    qkv_ref[:, 0:w] = head_norm(zcols(0, w), qg_ref).astype(BF16)
    qkv_ref[:, w:2 * w] = head_norm(zcols(w, 2 * w), kg_ref).astype(BF16)
    qkv_ref[:, 2 * w:3 * w] = zcols(2 * w, 3 * w).astype(BF16)
    o = 3 * w
    sc_x = zcols(o, o + 512)
    sc_b = zcols(o + 512, o + 1024)
    sc_c = zcols(o + 1024, o + 1536)
    scu_ref[:, 0:512] = sc_c * sc_x
    scu_ref[:, 512:1024] = sc_b
    cf_a = zcols(o + 1536, o + 2048)
    cf_gate = zcols(o + 2048, o + 2560)
    scu_ref[:, 1024:1536] = cf_a * jax.nn.sigmoid(cf_gate)
    o = o + 2560
    for c in range(0, 3 * D_MODEL, 512):
        gates_ref[:, c:c + 512] = jax.nn.sigmoid(zcols(o + c, o + c + 512)).astype(BF16)


def _inproj(x, lp, pending=None):
    n = x.shape[0]
    nt = n // TM
    params = [lp[k] for k in ("norm_mix_g", "w_in", "b_in", "qg", "kg", "bd")]
    x_spec = pl.BlockSpec((TM, D_MODEL), lambda i: (i, 0))
    param_specs = [_resident(a.shape) for a in params]
    out_specs = [pl.BlockSpec((TM, 3 * NA_WIDTH), lambda i: (i, 0)),
                 pl.BlockSpec((TM, 1536), lambda i: (i, 0)),
                 pl.BlockSpec((TM, 3 * D_MODEL), lambda i: (i, 0))]
    out_shape = [jax.ShapeDtypeStruct((n, 3 * NA_WIDTH), BF16),
                 jax.ShapeDtypeStruct((n, 1536), F32),
                 jax.ShapeDtypeStruct((n, 3 * D_MODEL), BF16)]
    if pending is None:
        return pl.pallas_call(
            _inproj_kernel, grid=(nt,), in_specs=[x_spec] + param_specs, out_specs=out_specs,
            out_shape=out_shape, compiler_params=_cparams("parallel"), name="inproj",
        )(x, *params)
    dest, ys = pending
    return pl.pallas_call(
        _inproj_gather_kernel,
        grid=(nt,),
        in_specs=_dest_specs(nt) + [x_spec, pl.BlockSpec(memory_space=pl.ANY)] + param_specs,
        out_specs=[x_spec] + out_specs,
        out_shape=[jax.ShapeDtypeStruct((n, D_MODEL), F32)] + out_shape,
        scratch_shapes=[pltpu.VMEM((2, TM * BLOCK_ROWS, 128), F32), pltpu.SemaphoreType.DMA((2,))],
        compiler_params=_cparams("arbitrary"),
        name="inproj_gather",
    )(dest, dest, x, ys, *params)


def _attn_kernel(rows, q_ref, kbuf, vbuf, t_ref, o_ref, s_scr, p_scr):
    r0 = pl.program_id(1) * RB
    w0 = jnp.clip(r0 - NA_KH // 2, 0, rows - (RB + NA_KH))
    lane = lax.broadcasted_iota(jnp.int32, (GRID_W, 128), 1)
    low = lane < NA_HEAD_DIM
    nkeys = NA_KH * GRID_W

    npair = NA_HEADS // 2

    def rows_body(ib, carry):
        geo = []
        for rr in range(ATTN_ROWS):
            i = ib * ATTN_ROWS + rr
            r = r0 + i
            rs = jnp.clip(r - NA_KH // 2, 0, rows - NA_KH)
            geo.append((r - rs, pl.multiple_of((rs - w0) * GRID_W, GRID_W), pl.multiple_of(i * GRID_W, GRID_W)))
        for rr, (d, start, qrow) in enumerate(geo):
            for j in range(npair):
                cols = slice(128 * j, 128 * (j + 1))
                qp = q_ref[pl.ds(qrow, GRID_W), cols]
                zero = jnp.zeros_like(qp)
                q2 = jnp.concatenate([jnp.where(low, qp, zero), jnp.where(low, zero, qp)], axis=0)
                kw = kbuf[pl.ds(start, nkeys), cols]
                s = lax.dot_general(q2, kw, (((1,), (1,)), ((), ())), preferred_element_type=F32)
                s_scr[rr * npair + j] = s + t_ref[j, d]
        for u in range(ATTN_ROWS * npair):
            s = s_scr[u]
            p = jnp.exp(s - jnp.max(s, axis=-1, keepdims=True))
            inv = 1.0 / jnp.sum(p, axis=-1, keepdims=True)
            p_scr[u] = (p * inv).astype(BF16)
        for rr, (d, start, qrow) in enumerate(geo):
            for j in range(npair):
                cols = slice(128 * j, 128 * (j + 1))
                vw = vbuf[pl.ds(start, nkeys), cols]
                o2 = jnp.dot(p_scr[rr * npair + j], vw, preferred_element_type=F32)
                o_ref[pl.ds(qrow, GRID_W), cols] = jnp.where(low, o2[0:GRID_W], o2[GRID_W:]).astype(BF16)
        return carry

    lax.fori_loop(0, RB // ATTN_ROWS, rows_body, 0)


def _attention(qkv, bn, seq_len, tbias):
    rows = seq_len // GRID_W
    win = RB + NA_KH
    assert rows >= win and rows % RB == 0
    nrb = rows // RB
    blk = RB * GRID_W
    n = bn * seq_len

    def window(col):
        def imap(b, i):
            start = b * seq_len + jnp.clip(i * RB - NA_KH // 2, 0, rows - win) * GRID_W
            return (pl.multiple_of(start, GRID_W), col * NA_WIDTH)
        return pl.BlockSpec((pl.Element(win * GRID_W), pl.Element(NA_WIDTH)), imap)

    return pl.pallas_call(
        functools.partial(_attn_kernel, rows),
        grid=(bn, nrb),
        in_specs=[pl.BlockSpec((blk, NA_WIDTH), lambda b, i: (b * nrb + i, 0)), window(1), window(2),
                  _resident(tbias.shape)],
        out_specs=pl.BlockSpec((blk, NA_WIDTH), lambda b, i: (b * nrb + i, 0)),
        out_shape=jax.ShapeDtypeStruct((n, NA_WIDTH), BF16),
        scratch_shapes=[pltpu.VMEM((ATTN_ROWS * NA_HEADS // 2, 2 * GRID_W, NA_KH * GRID_W), F32),
                        pltpu.VMEM((ATTN_ROWS * NA_HEADS // 2, 2 * GRID_W, NA_KH * GRID_W), BF16)],
        compiler_params=_cparams("parallel", "parallel"),
        name="nattn",
    )(qkv, qkv, qkv, tbias)


def _bias_table(rpb):
    cols = np.arange(GRID_W)
    cs = np.clip(cols - NA_KW // 2, 0, GRID_W - NA_KW)
    kc = np.arange(GRID_W)
    inside = (kc[None, :] >= cs[:, None]) & (kc[None, :] < cs[:, None] + NA_KW)
    co = kc[None, :] - cols[:, None] + NA_KW - 1
    sel = (co[:, :, None] == np.arange(2 * NA_KW - 1)[None, None, :]) & inside[:, :, None]
    t = jnp.einsum("hrx,ckx->hrck", rpb.astype(F32), jnp.asarray(sel, F32), precision=lax.Precision.HIGHEST)
    t = jnp.where(jnp.asarray(inside)[None, None], t, MASK_VALUE)
    t = jnp.stack([t[:, NA_KH - 1 - d:2 * NA_KH - 1 - d] for d in range(NA_KH)], axis=1)
    t = jnp.transpose(t, (0, 1, 3, 2, 4)).reshape(NA_HEADS // 2, 2, NA_KH, GRID_W, NA_KH * GRID_W)
    return jnp.transpose(t, (0, 2, 1, 3, 4)).reshape(NA_HEADS // 2, NA_KH, 2 * GRID_W, NA_KH * GRID_W)


def _conv_tile(t, nt_seq, scu_ref, hp_ref, hn_ref, scw_ref, cfw_ref, cfb_ref, cfg_ref, cfbeta_ref,
               ext_s, ext_u, cacc, ybc_ref):
    keep_p = t > 0
    keep_n = t < nt_seq - 1
    ext_s[0:HALO] = jnp.where(keep_p, hp_ref[:, 0:512].astype(F32), 0.0)
    ext_s[HALO:HALO + TM] = scu_ref[:, 0:512].astype(F32)
    ext_s[HALO + TM:] = jnp.where(keep_n, hn_ref[:, 0:512].astype(F32), 0.0)
    ext_u[0:HALO] = jnp.where(keep_p, hp_ref[:, 1024:1536].astype(F32), 0.0)
    ext_u[HALO:HALO + TM] = scu_ref[:, 1024:1536].astype(F32)
    ext_u[HALO + TM:] = jnp.where(keep_n, hn_ref[:, 1024:1536].astype(F32), 0.0)

    ch = CONV_CHUNK
    pad = CF_KERNEL // 2
    for cb in range(CF_WIDTH // 128):
        lanes = slice(128 * cb, 128 * (cb + 1))
        for ci in range(TM // ch):
            base = ci * ch
            acc = None
            for s in range(SUBLANES):
                ps = None
                for a in range((CF_KERNEL - s + SUBLANES - 1) // SUBLANES):
                    k = SUBLANES * a + s
                    lo = base + SUBLANES * a
                    term = ext_u[lo:lo + ch + SUBLANES, lanes] * cfw_ref[k:k + 1, lanes]
                    ps = term if ps is None else ps + term
                off = HALO - pad + s
                piece = ps[off:off + ch]
                acc = piece if acc is None else acc + piece
            cacc[base:base + ch, lanes] = acc
            win = ext_s[base + HALO - SUBLANES:base + HALO + ch + SUBLANES, lanes]
            accb = None
            for k in range(SC_KERNEL):
                off = SUBLANES - SC_KERNEL // 2 + k
                term = win[off:off + ch] * scw_ref[k:k + 1, lanes]
                accb = term if accb is None else accb + term
            scb = scu_ref[base:base + ch, 512 + 128 * cb:512 + 128 * (cb + 1)].astype(F32)
            ybc_ref[base:base + ch, lanes] = (scb * accb).astype(BF16)

    for ci in range(TM // ch):
        rows = slice(ci * ch, (ci + 1) * ch)
        acc = cacc[rows, :] + cfb_ref[...]
        mu = jnp.mean(acc, axis=-1, keepdims=True)
        xc = acc - mu
        var = jnp.mean(xc * xc, axis=-1, keepdims=True)
        y = xc * lax.rsqrt(var + EPS) * cfg_ref[...] + cfbeta_ref[...]
        ybc_ref[rows, SC_WIDTH:] = (y * jax.nn.sigmoid(y)).astype(BF16)


def _merge_kernel(nt_seq, x_ref, ya_ref, gates_ref, scu_ref, hp_ref, hn_ref, scw_ref, cfw_ref, cfb_ref, cfg_ref,
                  cfbeta_ref, wb_ref, wo_ref, o_ref, ext_s, ext_u, cacc, ybc):
    _conv_tile(lax.rem(pl.program_id(0), nt_seq), nt_seq, scu_ref, hp_ref, hn_ref, scw_ref, cfw_ref, cfb_ref,
               cfg_ref, cfbeta_ref, ext_s, ext_u, cacc, ybc)
    def branch(b, y):
        proj = jnp.dot(y, wb_ref[b], preferred_element_type=F32).astype(BF16)
        return gates_ref[:, b * D_MODEL:(b + 1) * D_MODEL] * proj

    merged = branch(0, ya_ref[...]) + branch(1, ybc[:, 0:SC_WIDTH]) + branch(2, ybc[:, SC_WIDTH:])
    o_ref[...] = x_ref[...] + jnp.dot(merged, wo_ref[...], preferred_element_type=F32)


def _merge(x, ya, scu, gates, seq_len, scw, cfw, cfb, cfg, cfbeta, wb, wo):
    n = x.shape[0]
    hb = TM // HALO

    def tile(width):
        return pl.BlockSpec((TM, width), lambda i: (i, 0))

    halo_prev = pl.BlockSpec((HALO, 1536), lambda i: (jnp.maximum(i * hb - 1, 0), 0))
    halo_next = pl.BlockSpec((HALO, 1536), lambda i: (jnp.minimum((i + 1) * hb, n // HALO - 1), 0))
    return pl.pallas_call(
        functools.partial(_merge_kernel, seq_len // TM),
        grid=(n // TM,),
        in_specs=[tile(D_MODEL), tile(NA_WIDTH), tile(3 * D_MODEL), tile(1536), halo_prev, halo_next,
                  _resident(scw.shape), _resident(cfw.shape), _resident(cfb.shape), _resident(cfg.shape),
                  _resident(cfbeta.shape), _resident(wb.shape), _resident(wo.shape)],
        out_specs=tile(D_MODEL),
        out_shape=jax.ShapeDtypeStruct((n, D_MODEL), F32),
        scratch_shapes=[pltpu.VMEM((TM + 2 * HALO, SC_WIDTH), F32), pltpu.VMEM((TM + 2 * HALO, CF_WIDTH), F32),
                        pltpu.VMEM((TM, CF_WIDTH), F32), pltpu.VMEM((TM, SC_WIDTH + CF_WIDTH), BF16)],
        compiler_params=_cparams("parallel"),
        name="merge",
    )(x, ya, gates, scu, scu, scu, scw, cfw, cfb, cfg, cfbeta, wb, wo)


def _first_index(hit, n):
    idx = jnp.full(hit[0].shape, n, jnp.int32)
    for i in range(n - 1, -1, -1):
        idx = jnp.where(hit[i], i, idx)
    return idx


def _router_kernel(x_ref, g_ref, wr_ref, br_ref, tri_ref, rows_ref, cls_ref, rank_ref, cnt_ref, carry):
    @pl.when(pl.program_id(0) == 0)
    def _():
        carry[...] = jnp.zeros_like(carry)

    x = x_ref[...]
    ms = jnp.mean(x * x, axis=-1, keepdims=True)
    h = x * lax.rsqrt(ms + EPS) * g_ref[...]
    h_hi = h.astype(BF16)
    h_hi32 = h_hi.astype(F32)
    bits = lax.bitcast_convert_type(h_hi32, jnp.int32)
    half = D_MODEL // 2
    for k in range(half // 128):
        lo = lax.shift_right_logical(bits[:, 128 * k:128 * (k + 1)], 16)
        hi = bits[:, half + 128 * k:half + 128 * (k + 1)] & HIGH_HALF
        rows_ref[pl.ds(k, TM, stride=BLOCK_ROWS), :] = hi | lo
    h_lo = (h - h_hi32).astype(BF16)
    nt_dims = (((1,), (1,)), ((), ()))
    part = lax.dot_general(wr_ref[...], h_hi, nt_dims, preferred_element_type=F32)
    logits = (part[0:CLS_ROWS] + part[CLS_ROWS:]
              + lax.dot_general(wr_ref[0:CLS_ROWS, :], h_lo, nt_dims, preferred_element_type=F32)
              + br_ref[:, 0:1])
    gl = [logits[g:g + 1, :] for g in range(MOE_GROUPS)]
    gmax = functools.reduce(jnp.maximum, gl)
    gidx = _first_index([gl[g] == gmax for g in range(MOE_GROUPS)], MOE_GROUPS)
    gval = 1.0 / functools.reduce(lambda a, b: a + b, [jnp.exp(v - gmax) for v in gl])
    esel = []
    for j in range(MOE_EPG):
        v = jnp.zeros_like(gmax)
        for g in range(MOE_GROUPS):
            row = MOE_GROUPS + g * MOE_EPG + j
            v = jnp.where(gidx == g, logits[row:row + 1, :], v)
        esel.append(v)
    v1 = functools.reduce(jnp.maximum, esel)
    i1 = _first_index([esel[j] == v1 for j in range(MOE_EPG)], MOE_EPG)
    rest = [jnp.where(i1 == j, -jnp.inf, esel[j]) for j in range(MOE_EPG)]
    v2 = functools.reduce(jnp.maximum, rest)
    i2 = _first_index([(rest[j] == v2) & (i1 != j) for j in range(MOE_EPG)], MOE_EPG)
    e2 = jnp.exp(v2 - v1)
    w1 = gval / (1.0 + e2)
    w2 = gval * e2 / (1.0 + e2)
    a = jnp.minimum(i1, i2)
    b = jnp.maximum(i1, i2)
    wa = jnp.where(i1 < i2, w1, w2)
    wb = jnp.where(i1 < i2, w2, w1)
    cls = gidx * N_PAIRS + (a * (2 * MOE_EPG - 1 - a)) // 2 + (b - a - 1)
    cls_ref[0] = cls

    crow = lax.broadcasted_iota(jnp.int32, (CLS_ROWS, TM), 0)
    onehot = (crow == cls).astype(F32)
    prefix = jnp.dot(onehot.astype(BF16), tri_ref[...], preferred_element_type=F32)
    before = carry[:, 0:1]
    rank = jnp.sum(onehot * (prefix - 1.0 + before), axis=0, keepdims=True)
    rank_ref[0] = rank.astype(jnp.int32)
    carry[...] = carry[...] + jnp.sum(onehot, axis=1, keepdims=True)
    cnt_ref[...] = carry[...]

    wrow = lax.broadcasted_iota(jnp.int32, (128, TM), 0)
    wmat = jnp.where(wrow == 0, wa, jnp.where(wrow == 1, wb, 0.0))
    rows_ref[pl.ds(W_ROW, TM, stride=BLOCK_ROWS), :] = lax.bitcast_convert_type(wmat.T, jnp.int32)
    for k in range(W_ROW + 1, BLOCK_ROWS):
        rows_ref[pl.ds(k, TM, stride=BLOCK_ROWS), :] = jnp.zeros((TM, 128), jnp.int32)


def _router(x, g, wr, br, tri):
    n = x.shape[0]
    nt = n // TM
    return pl.pallas_call(
        _router_kernel,
        grid=(nt,),
        in_specs=[pl.BlockSpec((TM, D_MODEL), lambda i: (i, 0)), _resident((1, D_MODEL)),
                  _resident(wr.shape), _resident(br.shape), _resident(tri.shape)],
        out_specs=[pl.BlockSpec((TM * BLOCK_ROWS, 128), lambda i: (i, 0)),
                   pl.BlockSpec((1, 1, TM), lambda i: (i, 0, 0)),
                   pl.BlockSpec((1, 1, TM), lambda i: (i, 0, 0)),
                   pl.BlockSpec((CLS_ROWS, 128), lambda i: (0, 0))],
        out_shape=[jax.ShapeDtypeStruct((n * BLOCK_ROWS, 128), jnp.int32),
                   jax.ShapeDtypeStruct((nt, 1, TM), jnp.int32),
                   jax.ShapeDtypeStruct((nt, 1, TM), jnp.int32),
                   jax.ShapeDtypeStruct((CLS_ROWS, 128), F32)],
        scratch_shapes=[pltpu.VMEM((CLS_ROWS, 128), F32)],
        compiler_params=_cparams("arbitrary"),
        name="router",
    )(x, g, wr, br, tri)


def _block(ref, row):
    return ref.at[pl.ds(pl.multiple_of(row * BLOCK_ROWS, BLOCK_ROWS), BLOCK_ROWS)]


def _start_rows(copy):
    def group(g, c):
        for u in range(DMA_UNROLL):
            copy(g * DMA_UNROLL + u).start(priority=u % 2)
        return c

    lax.fori_loop(0, TM // DMA_UNROLL, group, 0)


def _wait_rows(copy):
    def wait(j, c):
        copy(j).wait()
        return c

    lax.fori_loop(0, TM, wait, 0, unroll=DMA_UNROLL)


def _scatter_kernel(tme, ztile_ref, dest_ref, rows_ref, hs_ref, zeros, sem, zsem):
    @pl.when(pl.program_id(0) == 0)
    def _():
        zeros[...] = jnp.zeros_like(zeros)

        def zcopy(c):
            start = pl.multiple_of(ztile_ref[c] * (tme * BLOCK_ROWS), BLOCK_ROWS)
            return pltpu.make_async_copy(zeros, hs_ref.at[pl.ds(start, tme * BLOCK_ROWS)], zsem)

        for c in range(2 * N_CLASSES):
            @pl.when(ztile_ref[c] >= 0)
            def _():
                zcopy(c).start()
        for c in range(2 * N_CLASSES):
            @pl.when(ztile_ref[c] >= 0)
            def _():
                zcopy(c).wait()

    def copy(j):
        return pltpu.make_async_copy(_block(rows_ref, j), _block(hs_ref, dest_ref[0, 0, j]), sem)

    _start_rows(copy)
    _wait_rows(copy)


def _scatter(zero_tiles, dest, rows, n_sorted, tme):
    nt = dest.shape[0]
    return pl.pallas_call(
        functools.partial(_scatter_kernel, tme),
        grid_spec=pltpu.PrefetchScalarGridSpec(
            num_scalar_prefetch=1,
            grid=(nt,),
            in_specs=[pl.BlockSpec((1, 1, TM), lambda i, z: (i, 0, 0), memory_space=pltpu.SMEM),
                      pl.BlockSpec((TM * BLOCK_ROWS, 128), lambda i, z: (i, 0))],
            out_specs=pl.BlockSpec(memory_space=pl.ANY),
            scratch_shapes=[pltpu.VMEM((tme * BLOCK_ROWS, 128), jnp.int32), pltpu.SemaphoreType.DMA(()),
                            pltpu.SemaphoreType.DMA(())],
        ),
        out_shape=jax.ShapeDtypeStruct((n_sorted * BLOCK_ROWS, 128), jnp.int32),
        compiler_params=_cparams("arbitrary"),
        name="row_scatter",
    )(zero_tiles, dest, rows)


def _fetch_token_blocks(dest_ref, dest_next_ref, ys_ref, buf, sem):
    i = pl.program_id(0)
    slot = lax.rem(i, 2)

    def copy(d_ref, s, j):
        return pltpu.make_async_copy(_block(ys_ref, d_ref[0, 0, j]), _block(buf.at[s], j), sem.at[s])

    @pl.when(i == 0)
    def _():
        _start_rows(functools.partial(copy, dest_ref, slot))

    @pl.when(i + 1 < pl.num_programs(0))
    def _():
        _start_rows(functools.partial(copy, dest_next_ref, 1 - slot))

    _wait_rows(functools.partial(copy, dest_ref, slot))
    return slot


def _dest_specs(nt):
    return [pl.BlockSpec((1, 1, TM), lambda i: (i, 0, 0), memory_space=pltpu.SMEM),
            pl.BlockSpec((1, 1, TM), lambda i: (jnp.minimum(i + 1, nt - 1), 0, 0), memory_space=pltpu.SMEM)]


def _gather_kernel(dest_ref, dest_next_ref, x_ref, ys_ref, o_ref, buf, sem):
    slot = _fetch_token_blocks(dest_ref, dest_next_ref, ys_ref, buf, sem)
    for k in range(D_MODEL // 128):
        lanes = slice(128 * k, 128 * (k + 1))
        o_ref[:, lanes] = x_ref[:, lanes] + buf[slot, pl.ds(k, TM, stride=BLOCK_ROWS), :]


def _gather_residual(dest, x, ys):
    n = x.shape[0]
    nt = n // TM
    return pl.pallas_call(
        _gather_kernel,
        grid=(nt,),
        in_specs=_dest_specs(nt) + [pl.BlockSpec((TM, D_MODEL), lambda i: (i, 0)),
                                    pl.BlockSpec(memory_space=pl.ANY)],
        out_specs=pl.BlockSpec((TM, D_MODEL), lambda i: (i, 0)),
        scratch_shapes=[pltpu.VMEM((2, TM * BLOCK_ROWS, 128), F32), pltpu.SemaphoreType.DMA((2,))],
        out_shape=jax.ShapeDtypeStruct((n, D_MODEL), F32),
        compiler_params=_cparams("arbitrary"),
        name="row_gather",
    )(dest, dest, x, ys)


def _expert_kernel(tme, e1_ref, e2_ref, src_ref, hs_ref, wgu1_ref, wgu2_ref, wd1_ref, wd2_ref, ys_ref):
    i = pl.program_id(0)

    @pl.when(src_ref[i] == i)
    def _():
        words = [hs_ref[pl.ds(k, tme, stride=BLOCK_ROWS), :] for k in range(D_MODEL // 256)]
        lo = [lax.bitcast_convert_type(lax.shift_left(w, 16), F32).astype(BF16) for w in words]
        hi = [lax.bitcast_convert_type(w & HIGH_HALF, F32).astype(BF16) for w in words]
        h = jnp.concatenate(lo + hi, axis=1)
        wts = lax.bitcast_convert_type(hs_ref[pl.ds(W_ROW, tme, stride=BLOCK_ROWS), :], F32)

        def expert(wgu_ref, wd_ref):
            gu = jnp.dot(h, wgu_ref[0], preferred_element_type=F32)
            gate = gu[:, 0:D_EXPERT]
            act = gate * jax.nn.sigmoid(gate) * gu[:, D_EXPERT:]
            return jnp.dot(act.astype(BF16), wd_ref[0], preferred_element_type=F32)

        y = wts[:, 0:1] * expert(wgu1_ref, wd1_ref) + wts[:, 1:2] * expert(wgu2_ref, wd2_ref)
        for k in range(D_MODEL // 128):
            ys_ref[pl.ds(k, tme, stride=BLOCK_ROWS), :] = y[:, 128 * k:128 * (k + 1)]

    @pl.when(src_ref[i] != i)
    def _():
        ys_ref[...] = jnp.zeros_like(ys_ref)


def _experts(tile_e1, tile_e2, tile_src, hs, wgu, wd, tme):
    n_sorted = hs.shape[0] // BLOCK_ROWS
    nt = n_sorted // tme
    return pl.pallas_call(
        functools.partial(_expert_kernel, tme),
        grid_spec=pltpu.PrefetchScalarGridSpec(
            num_scalar_prefetch=3,
            grid=(nt,),
            in_specs=[
                pl.BlockSpec((tme * BLOCK_ROWS, 128), lambda i, e1, e2, src: (src[i], 0)),
                pl.BlockSpec((1, D_MODEL, 2 * D_EXPERT), lambda i, e1, e2, src: (e1[i], 0, 0)),
                pl.BlockSpec((1, D_MODEL, 2 * D_EXPERT), lambda i, e1, e2, src: (e2[i], 0, 0)),
                pl.BlockSpec((1, D_EXPERT, D_MODEL), lambda i, e1, e2, src: (e1[i], 0, 0)),
                pl.BlockSpec((1, D_EXPERT, D_MODEL), lambda i, e1, e2, src: (e2[i], 0, 0)),
            ],
            out_specs=pl.BlockSpec((tme * BLOCK_ROWS, 128), lambda i, e1, e2, src: (i, 0)),
        ),
        out_shape=jax.ShapeDtypeStruct((n_sorted * BLOCK_ROWS, 128), F32),
        compiler_params=_cparams("arbitrary"),
        name="experts",
    )(tile_e1, tile_e2, tile_src, hs, wgu, wgu, wd, wd)


_PAIR_A = np.array([a for a in range(MOE_EPG) for b in range(a + 1, MOE_EPG)], np.int32)
_PAIR_B = np.array([b for a in range(MOE_EPG) for b in range(a + 1, MOE_EPG)], np.int32)


def _moe(x, lp):
    n = x.shape[0]
    rows, cls, rank, counts = _router(x, lp["norm_ffn_g"], lp["wr"], lp["br"], lp["tri"])
    tme = TME_BIG if n >= 4 * N_CLASSES * TME_BIG else TME
    n_tiles = n // tme + N_CLASSES
    cnt = counts[:N_CLASSES, 0].astype(jnp.int32)
    tiles_per_class = (cnt + tme - 1) // tme
    tile_end = jnp.cumsum(tiles_per_class)
    seg_start = (tile_end - tiles_per_class) * tme
    last_tile = jnp.where(tiles_per_class > 0, tile_end - 1, -1)
    tile_id = jnp.arange(n_tiles, dtype=jnp.int32)
    tile_cls = jnp.sum((tile_id[:, None] >= tile_end[None, :]).astype(jnp.int32), axis=1)
    tile_src = jnp.minimum(tile_id, tile_end[-1] - 1)
    tile_cls = jnp.minimum(tile_cls, N_CLASSES - 1)
    class_id = jnp.arange(N_CLASSES, dtype=jnp.int32)
    class_np = np.arange(N_CLASSES)
    class_e1 = jnp.asarray((class_np // N_PAIRS) * MOE_EPG + _PAIR_A[class_np % N_PAIRS], jnp.int32)
    class_e2 = jnp.asarray((class_np // N_PAIRS) * MOE_EPG + _PAIR_B[class_np % N_PAIRS], jnp.int32)
    tile_hot = (tile_cls[:, None] == class_id[None, :]).astype(jnp.int32)
    tile_e1 = jnp.sum(tile_hot * class_e1[None, :], axis=1)
    tile_e2 = jnp.sum(tile_hot * class_e2[None, :], axis=1)
    dest = rank + jnp.sum(jnp.where(cls[..., None] == class_id, seg_start, 0), axis=-1)
    spare = tile_end[-1] + jnp.arange(N_CLASSES, dtype=jnp.int32)
    zero_tiles = jnp.concatenate([last_tile, jnp.where(spare < n_tiles, spare, -1)]).astype(jnp.int32)
    hs = _scatter(zero_tiles, dest, rows, n_tiles * tme, tme)
    return dest, _experts(tile_e1, tile_e2, tile_src, hs, lp["wgu"], lp["wd"], tme)


def _prep_layer(l, norm_mix_g, w_in, b_in, q_norm_g, k_norm_g, na_rpb, sc_conv_w, cf_conv_w, cf_conv_b,
                cf_norm_g, cf_norm_b, w_branch, w_out, norm_ffn_g, router_group_w, router_group_b,
                router_expert_w, router_expert_b, expert_w_gate, expert_w_up, expert_w_down):
    head = np.arange(NA_WIDTH) // NA_HEAD_DIM
    bd = jnp.asarray((head[:, None] == head[None, :]).astype(np.float32) / NA_HEAD_DIM, BF16)
    wr = jnp.zeros((CLS_ROWS, D_MODEL), F32)
    wr = wr.at[0:MOE_GROUPS].set(router_group_w[l].T.astype(F32))
    wr = wr.at[MOE_GROUPS:MOE_GROUPS + MOE_EXPERTS].set(router_expert_w[l].T.astype(F32))
    br = jnp.zeros((CLS_ROWS,), F32)
    br = br.at[0:MOE_GROUPS].set(router_group_b[l].astype(F32))
    br = br.at[MOE_GROUPS:MOE_GROUPS + MOE_EXPERTS].set(router_expert_b[l].astype(F32))
    tri = jnp.asarray(np.triu(np.ones((TM, TM), np.float32)), BF16)
    return dict(
        norm_mix_g=norm_mix_g[l].reshape(1, D_MODEL).astype(F32),
        w_in=w_in[l].astype(BF16),
        b_in=b_in[l].reshape(1, IN_COLS).astype(F32),
        qg=(jnp.tile(q_norm_g[l].astype(F32), NA_HEADS) * (NA_HEAD_DIM ** -0.5)).reshape(1, NA_WIDTH),
        kg=jnp.tile(k_norm_g[l].astype(F32), NA_HEADS).reshape(1, NA_WIDTH),
        bd=bd,
        tbias=_bias_table(na_rpb[l]),
        scw=sc_conv_w[l].astype(F32),
        cfw=cf_conv_w[l].astype(F32),
        cfb=cf_conv_b[l].reshape(1, CF_WIDTH).astype(F32),
        cfg=cf_norm_g[l].reshape(1, CF_WIDTH).astype(F32),
        cfbeta=cf_norm_b[l].reshape(1, CF_WIDTH).astype(F32),
        wb=w_branch[l].astype(BF16),
        wo=w_out[l].astype(BF16),
        norm_ffn_g=norm_ffn_g[l].reshape(1, D_MODEL).astype(F32),
        wr=jnp.concatenate([wr.astype(BF16), (wr - wr.astype(BF16).astype(F32)).astype(BF16)], axis=0),
        br=jnp.broadcast_to(br[:, None], (CLS_ROWS, 128)),
        tri=tri,
        wgu=jnp.concatenate([expert_w_gate[l], expert_w_up[l]], axis=-1).astype(BF16),
        wd=expert_w_down[l].astype(BF16),
    )


def _trunk(x, layers):
    bn, seq_len, _ = x.shape
    assert seq_len % TM == 0 and seq_len % (RB * GRID_W) == 0
    xf = x.reshape(bn * seq_len, D_MODEL)
    pending = None
    for lp in layers:
        if pending is None:
            qkv, scu, gates = _inproj(xf, lp)
        else:
            xf, qkv, scu, gates = _inproj(xf, lp, pending)
        ya = _attention(qkv, bn, seq_len, lp["tbias"])
        xf = _merge(xf, ya, scu, gates, seq_len, lp["scw"], lp["cfw"], lp["cfb"], lp["cfg"], lp["cfbeta"],
                    lp["wb"], lp["wo"])
        pending = _moe(xf, lp)
    return _gather_residual(pending[0], xf, pending[1]).reshape(bn, seq_len, D_MODEL)


def kernel(x_prompt, x_sample, norm_mix_g, w_in, b_in, q_norm_g, k_norm_g, na_rpb, sc_conv_w, cf_conv_w, cf_conv_b, cf_norm_g, cf_norm_b, w_branch, w_out, norm_ffn_g, router_group_w, router_group_b, router_expert_w, router_expert_b, expert_w_gate, expert_w_up, expert_w_down):
    params = (norm_mix_g, w_in, b_in, q_norm_g, k_norm_g, na_rpb, sc_conv_w, cf_conv_w, cf_conv_b, cf_norm_g,
              cf_norm_b, w_branch, w_out, norm_ffn_g, router_group_w, router_group_b, router_expert_w,
              router_expert_b, expert_w_gate, expert_w_up, expert_w_down)
    layers = [_prep_layer(l, *params) for l in range(norm_mix_g.shape[0])]
    return (_trunk(x_prompt, layers), _trunk(x_sample, layers))
```

```python
import functools

import numpy as np
import jax
import jax.numpy as jnp
from jax import lax
from jax.experimental import pallas as pl
from jax.experimental.pallas import tpu as pltpu

F32 = jnp.float32
BF16 = jnp.bfloat16

D_MODEL = 1024
GRID_W = 64
NA_HEADS = 8
NA_HEAD_DIM = 64
NA_WIDTH = NA_HEADS * NA_HEAD_DIM
NA_KH = 8
NA_KW = 16
SC_WIDTH = 512
SC_KERNEL = 3
CF_WIDTH = 512
CF_KERNEL = 31
IN_COLS = 3 * NA_WIDTH + 3 * SC_WIDTH + 2 * CF_WIDTH + 3 * D_MODEL
MOE_GROUPS = 4
MOE_EPG = 4
MOE_EXPERTS = MOE_GROUPS * MOE_EPG
D_EXPERT = 256
EPS = 1e-6

N_PAIRS = MOE_EPG * (MOE_EPG - 1) // 2
N_CLASSES = MOE_GROUPS * N_PAIRS
CLS_ROWS = 32
BLOCK_ROWS = 8
W_ROW = 4
HIGH_HALF = -65536
DMA_UNROLL = 8
MASK_VALUE = -1e30
LOG2E = 1.4426950408889634

TM = 512
RB = 8
ATTN_ROWS = 4
HALO = 16
CONV_CHUNK = 128
SUBLANES = 8
TME = 256
TME_BIG = 512
VMEM_LIMIT = 56 * 1024 * 1024


def _cparams(*sem):
    return pltpu.CompilerParams(dimension_semantics=sem, vmem_limit_bytes=VMEM_LIMIT)


def _resident(shape):
    nd = len(shape)
    return pl.BlockSpec(shape, lambda *_: (0,) * nd, pipeline_mode=pl.Buffered(1))


def _inproj_kernel(x_ref, g_ref, w_ref, b_ref, qg_ref, kg_ref, bd_ref, qkv_ref, scu_ref, gates_ref):
    _inproj_body(x_ref[...], g_ref, w_ref, b_ref, qg_ref, kg_ref, bd_ref, qkv_ref, scu_ref, gates_ref)


def _inproj_gather_kernel(dest_ref, dest_next_ref, x_ref, ys_ref, g_ref, w_ref, b_ref, qg_ref, kg_ref, bd_ref,
                          xo_ref, qkv_ref, scu_ref, gates_ref, buf, sem):
    slot = _fetch_token_blocks(dest_ref, dest_next_ref, ys_ref, buf, sem)
    for k in range(D_MODEL // 128):
        lanes = slice(128 * k, 128 * (k + 1))
        xo_ref[:, lanes] = x_ref[:, lanes] + buf[slot, pl.ds(k, TM, stride=BLOCK_ROWS), :]
    _inproj_body(xo_ref[...], g_ref, w_ref, b_ref, qg_ref, kg_ref, bd_ref, qkv_ref, scu_ref, gates_ref)


def _inproj_body(x, g_ref, w_ref, b_ref, qg_ref, kg_ref, bd_ref, qkv_ref, scu_ref, gates_ref):
    ms = jnp.mean(x * x, axis=-1, keepdims=True)
    h = (x * lax.rsqrt(ms + EPS) * g_ref[...]).astype(BF16)

    def zcols(c0, c1):
        return jnp.dot(h, w_ref[:, c0:c1], preferred_element_type=F32) + b_ref[:, c0:c1]

    def head_norm(z, gain_ref):
        hm = jnp.dot((z * z).astype(BF16), bd_ref[...], preferred_element_type=F32)
        return z * lax.rsqrt(hm + EPS) * gain_ref[...]

    w = NA_WIDTH
    qkv_ref[:, 0:w] = head_norm(zcols(0, w), qg_ref).astype(BF16)
    qkv_ref[:, w:2 * w] = head_norm(zcols(w, 2 * w), kg_ref).astype(BF16)
    qkv_ref[:, 2 * w:3 * w] = zcols(2 * w, 3 * w).astype(BF16)
    o = 3 * w
    sc_x = zcols(o, o + 512)
    sc_b = zcols(o + 512, o + 1024)
    sc_c = zcols(o + 1024, o + 1536)
    scu_ref[:, 0:512] = sc_c * sc_x
    scu_ref[:, 512:1024] = sc_b
    cf_a = zcols(o + 1536, o + 2048)
    cf_gate = zcols(o + 2048, o + 2560)
    scu_ref[:, 1024:1536] = cf_a * jax.nn.sigmoid(cf_gate)
    o = o + 2560
    for c in range(0, 3 * D_MODEL, 512):
        gates_ref[:, c:c + 512] = jax.nn.sigmoid(zcols(o + c, o + c + 512)).astype(BF16)


def _inproj(x, lp, pending=None):
    n = x.shape[0]
    nt = n // TM
    params = [lp[k] for k in ("norm_mix_g", "w_in", "b_in", "qg", "kg", "bd")]
    x_spec = pl.BlockSpec((TM, D_MODEL), lambda i: (i, 0))
    param_specs = [_resident(a.shape) for a in params]
    out_specs = [pl.BlockSpec((TM, 3 * NA_WIDTH), lambda i: (i, 0)),
                 pl.BlockSpec((TM, 1536), lambda i: (i, 0)),
                 pl.BlockSpec((TM, 3 * D_MODEL), lambda i: (i, 0))]
    out_shape = [jax.ShapeDtypeStruct((n, 3 * NA_WIDTH), BF16),
                 jax.ShapeDtypeStruct((n, 1536), F32),
                 jax.ShapeDtypeStruct((n, 3 * D_MODEL), BF16)]
    if pending is None:
        return pl.pallas_call(
            _inproj_kernel, grid=(nt,), in_specs=[x_spec] + param_specs, out_specs=out_specs,
            out_shape=out_shape, compiler_params=_cparams("parallel"), name="inproj",
        )(x, *params)
    dest, ys = pending
    return pl.pallas_call(
        _inproj_gather_kernel,
        grid=(nt,),
        in_specs=_dest_specs(nt) + [x_spec, pl.BlockSpec(memory_space=pl.ANY)] + param_specs,
        out_specs=[x_spec] + out_specs,
        out_shape=[jax.ShapeDtypeStruct((n, D_MODEL), F32)] + out_shape,
        scratch_shapes=[pltpu.VMEM((2, TM * BLOCK_ROWS, 128), F32), pltpu.SemaphoreType.DMA((2,))],
        compiler_params=_cparams("arbitrary"),
        name="inproj_gather",
    )(dest, dest, x, ys, *params)


def _attn_kernel(rows, q_ref, kbuf, vbuf, t_ref, o_ref, s_scr, p_scr):
    r0 = pl.program_id(1) * RB
    w0 = jnp.clip(r0 - NA_KH // 2, 0, rows - (RB + NA_KH))
    lane = lax.broadcasted_iota(jnp.int32, (GRID_W, 128), 1)
    low = lane < NA_HEAD_DIM
    nkeys = NA_KH * GRID_W

    npair = NA_HEADS // 2

    def rows_body(ib, carry):
        geo = []
        for rr in range(ATTN_ROWS):
            i = ib * ATTN_ROWS + rr
            r = r0 + i
            rs = jnp.clip(r - NA_KH // 2, 0, rows - NA_KH)
            geo.append((r - rs, pl.multiple_of((rs - w0) * GRID_W, GRID_W), pl.multiple_of(i * GRID_W, GRID_W)))
        for rr, (d, start, qrow) in enumerate(geo):
            for j in range(npair):
                cols = slice(128 * j, 128 * (j + 1))
                qp = q_ref[pl.ds(qrow, GRID_W), cols]
                zero = jnp.zeros_like(qp)
                q2 = jnp.concatenate([jnp.where(low, qp, zero), jnp.where(low, zero, qp)], axis=0)
                kw = kbuf[pl.ds(start, nkeys), cols]
                s = lax.dot_general(q2, kw, (((1,), (1,)), ((), ())), preferred_element_type=F32)
                s_scr[rr * npair + j] = s + t_ref[j, d]
        for u in range(ATTN_ROWS * npair):
            s = s_scr[u]
            p = jnp.exp2(s - jnp.max(s, axis=-1, keepdims=True))
            inv = 1.0 / jnp.sum(p, axis=-1, keepdims=True)
            p_scr[u] = (p * inv).astype(BF16)
        for rr, (d, start, qrow) in enumerate(geo):
            for j in range(npair):
                cols = slice(128 * j, 128 * (j + 1))
                vw = vbuf[pl.ds(start, nkeys), cols]
                o2 = jnp.dot(p_scr[rr * npair + j], vw, preferred_element_type=F32)
                o_ref[pl.ds(qrow, GRID_W), cols] = jnp.where(low, o2[0:GRID_W], o2[GRID_W:]).astype(BF16)
        return carry

    lax.fori_loop(0, RB // ATTN_ROWS, rows_body, 0)


def _attention(qkv, bn, seq_len, tbias):
    rows = seq_len // GRID_W
    win = RB + NA_KH
    assert rows >= win and rows % RB == 0
    nrb = rows // RB
    blk = RB * GRID_W
    n = bn * seq_len

    def window(col):
        def imap(b, i):
            start = b * seq_len + jnp.clip(i * RB - NA_KH // 2, 0, rows - win) * GRID_W
            return (pl.multiple_of(start, GRID_W), col * NA_WIDTH)
        return pl.BlockSpec((pl.Element(win * GRID_W), pl.Element(NA_WIDTH)), imap)

    return pl.pallas_call(
        functools.partial(_attn_kernel, rows),
        grid=(bn, nrb),
        in_specs=[pl.BlockSpec((blk, NA_WIDTH), lambda b, i: (b * nrb + i, 0)), window(1), window(2),
                  _resident(tbias.shape)],
        out_specs=pl.BlockSpec((blk, NA_WIDTH), lambda b, i: (b * nrb + i, 0)),
        out_shape=jax.ShapeDtypeStruct((n, NA_WIDTH), BF16),
        scratch_shapes=[pltpu.VMEM((ATTN_ROWS * NA_HEADS // 2, 2 * GRID_W, NA_KH * GRID_W), F32),
                        pltpu.VMEM((ATTN_ROWS * NA_HEADS // 2, 2 * GRID_W, NA_KH * GRID_W), BF16)],
        compiler_params=_cparams("parallel", "parallel"),
        name="nattn",
    )(qkv, qkv, qkv, tbias)


def _bias_table(rpb):
    cols = np.arange(GRID_W)
    cs = np.clip(cols - NA_KW // 2, 0, GRID_W - NA_KW)
    kc = np.arange(GRID_W)
    inside = (kc[None, :] >= cs[:, None]) & (kc[None, :] < cs[:, None] + NA_KW)
    co = kc[None, :] - cols[:, None] + NA_KW - 1
    sel = (co[:, :, None] == np.arange(2 * NA_KW - 1)[None, None, :]) & inside[:, :, None]
    t = jnp.einsum("hrx,ckx->hrck", rpb.astype(F32), jnp.asarray(sel, F32), precision=lax.Precision.HIGHEST)
    t = jnp.where(jnp.asarray(inside)[None, None], t * LOG2E, MASK_VALUE)
    t = jnp.stack([t[:, NA_KH - 1 - d:2 * NA_KH - 1 - d] for d in range(NA_KH)], axis=1)
    t = jnp.transpose(t, (0, 1, 3, 2, 4)).reshape(NA_HEADS // 2, 2, NA_KH, GRID_W, NA_KH * GRID_W)
    return jnp.transpose(t, (0, 2, 1, 3, 4)).reshape(NA_HEADS // 2, NA_KH, 2 * GRID_W, NA_KH * GRID_W)


def _rows_from(v, off, n):
    first, r = divmod(off, SUBLANES)
    if r == 0:
        return v[off:off + n]
    nt = n // SUBLANES
    tiles = [pltpu.roll(v[SUBLANES * (first + j):SUBLANES * (first + j + 1)], SUBLANES - r, axis=0)
             for j in range(nt + 1)]
    head = lax.broadcasted_iota(jnp.int32, (SUBLANES, 128), 0) < SUBLANES - r
    return jnp.concatenate([jnp.where(head, tiles[j], tiles[j + 1]) for j in range(nt)], axis=0)


def _conv_tile(t, nt_seq, scu_ref, hp_ref, hn_ref, scw_ref, cfw_ref, cfb_ref, cfg_ref, cfbeta_ref,
               ext_s, ext_u, cacc, ybc_ref):
    keep_p = t > 0
    keep_n = t < nt_seq - 1
    ext_s[0:HALO] = jnp.where(keep_p, hp_ref[:, 0:512].astype(F32), 0.0)
    ext_s[HALO:HALO + TM] = scu_ref[:, 0:512].astype(F32)
    ext_s[HALO + TM:] = jnp.where(keep_n, hn_ref[:, 0:512].astype(F32), 0.0)
    ext_u[0:HALO] = jnp.where(keep_p, hp_ref[:, 1024:1536].astype(F32), 0.0)
    ext_u[HALO:HALO + TM] = scu_ref[:, 1024:1536].astype(F32)
    ext_u[HALO + TM:] = jnp.where(keep_n, hn_ref[:, 1024:1536].astype(F32), 0.0)

    ch = CONV_CHUNK
    pad = CF_KERNEL // 2
    for cb in range(CF_WIDTH // 128):
        lanes = slice(128 * cb, 128 * (cb + 1))
        for ci in range(TM // ch):
            base = ci * ch
            acc = None
            for s in range(SUBLANES):
                ps = None
                for a in range((CF_KERNEL - s + SUBLANES - 1) // SUBLANES):
                    k = SUBLANES * a + s
                    lo = base + SUBLANES * a
                    term = ext_u[lo:lo + ch + SUBLANES, lanes] * cfw_ref[k:k + 1, lanes]
                    ps = term if ps is None else ps + term
                piece = _rows_from(ps, HALO - pad + s, ch)
                acc = piece if acc is None else acc + piece
            cacc[base:base + ch, lanes] = acc
            win = ext_s[base + HALO - SUBLANES:base + HALO + ch + SUBLANES, lanes]
            accb = None
            for k in range(SC_KERNEL):
                off = SUBLANES - SC_KERNEL // 2 + k
                term = _rows_from(win, off, ch) * scw_ref[k:k + 1, lanes]
                accb = term if accb is None else accb + term
            scb = scu_ref[base:base + ch, 512 + 128 * cb:512 + 128 * (cb + 1)].astype(F32)
            ybc_ref[base:base + ch, lanes] = (scb * accb).astype(BF16)

    for ci in range(TM // ch):
        rows = slice(ci * ch, (ci + 1) * ch)
        acc = cacc[rows, :] + cfb_ref[...]
        mu = jnp.mean(acc, axis=-1, keepdims=True)
        xc = acc - mu
        var = jnp.mean(xc * xc, axis=-1, keepdims=True)
        y = xc * lax.rsqrt(var + EPS) * cfg_ref[...] + cfbeta_ref[...]
        ybc_ref[rows, SC_WIDTH:] = (y * jax.nn.sigmoid(y)).astype(BF16)


def _merge_kernel(nt_seq, x_ref, ya_ref, gates_ref, scu_ref, hp_ref, hn_ref, scw_ref, cfw_ref, cfb_ref, cfg_ref,
                  cfbeta_ref, wb_ref, wo_ref, o_ref, ext_s, ext_u, cacc, ybc):
    _conv_tile(lax.rem(pl.program_id(0), nt_seq), nt_seq, scu_ref, hp_ref, hn_ref, scw_ref, cfw_ref, cfb_ref,
               cfg_ref, cfbeta_ref, ext_s, ext_u, cacc, ybc)
    def branch(b, y):
        proj = jnp.dot(y, wb_ref[b], preferred_element_type=F32).astype(BF16)
        return gates_ref[:, b * D_MODEL:(b + 1) * D_MODEL] * proj

    merged = branch(0, ya_ref[...]) + branch(1, ybc[:, 0:SC_WIDTH]) + branch(2, ybc[:, SC_WIDTH:])
    o_ref[...] = x_ref[...] + jnp.dot(merged, wo_ref[...], preferred_element_type=F32)


def _merge(x, ya, scu, gates, seq_len, scw, cfw, cfb, cfg, cfbeta, wb, wo):
    n = x.shape[0]
    hb = TM // HALO

    def tile(width):
        return pl.BlockSpec((TM, width), lambda i: (i, 0))

    halo_prev = pl.BlockSpec((HALO, 1536), lambda i: (jnp.maximum(i * hb - 1, 0), 0))
    halo_next = pl.BlockSpec((HALO, 1536), lambda i: (jnp.minimum((i + 1) * hb, n // HALO - 1), 0))
    return pl.pallas_call(
        functools.partial(_merge_kernel, seq_len // TM),
        grid=(n // TM,),
        in_specs=[tile(D_MODEL), tile(NA_WIDTH), tile(3 * D_MODEL), tile(1536), halo_prev, halo_next,
                  _resident(scw.shape), _resident(cfw.shape), _resident(cfb.shape), _resident(cfg.shape),
                  _resident(cfbeta.shape), _resident(wb.shape), _resident(wo.shape)],
        out_specs=tile(D_MODEL),
        out_shape=jax.ShapeDtypeStruct((n, D_MODEL), F32),
        scratch_shapes=[pltpu.VMEM((TM + 2 * HALO, SC_WIDTH), F32), pltpu.VMEM((TM + 2 * HALO, CF_WIDTH), F32),
                        pltpu.VMEM((TM, CF_WIDTH), F32), pltpu.VMEM((TM, SC_WIDTH + CF_WIDTH), BF16)],
        compiler_params=_cparams("parallel"),
        name="merge",
    )(x, ya, gates, scu, scu, scu, scw, cfw, cfb, cfg, cfbeta, wb, wo)


def _first_index(hit, n):
    idx = jnp.full(hit[0].shape, n, jnp.int32)
    for i in range(n - 1, -1, -1):
        idx = jnp.where(hit[i], i, idx)
    return idx


def _router_kernel(x_ref, g_ref, wr_ref, br_ref, tri_ref, rows_ref, cls_ref, rank_ref, cnt_ref, carry):
    @pl.when(pl.program_id(0) == 0)
    def _():
        carry[...] = jnp.zeros_like(carry)

    x = x_ref[...]
    ms = jnp.mean(x * x, axis=-1, keepdims=True)
    h = x * lax.rsqrt(ms + EPS) * g_ref[...]
    h_hi = h.astype(BF16)
    h_hi32 = h_hi.astype(F32)
    bits = lax.bitcast_convert_type(h_hi32, jnp.int32)
    half = D_MODEL // 2
    for k in range(half // 128):
        lo = lax.shift_right_logical(bits[:, 128 * k:128 * (k + 1)], 16)
        hi = bits[:, half + 128 * k:half + 128 * (k + 1)] & HIGH_HALF
        rows_ref[pl.ds(k, TM, stride=BLOCK_ROWS), :] = hi | lo
    h_lo = (h - h_hi32).astype(BF16)
    nt_dims = (((1,), (1,)), ((), ()))
    part = lax.dot_general(wr_ref[...], h_hi, nt_dims, preferred_element_type=F32)
    logits = (part[0:CLS_ROWS] + part[CLS_ROWS:]
              + lax.dot_general(wr_ref[0:CLS_ROWS, :], h_lo, nt_dims, preferred_element_type=F32)
              + br_ref[:, 0:1])
    gl = [logits[g:g + 1, :] for g in range(MOE_GROUPS)]
    gmax = functools.reduce(jnp.maximum, gl)
    gidx = _first_index([gl[g] == gmax for g in range(MOE_GROUPS)], MOE_GROUPS)
    gval = 1.0 / functools.reduce(lambda a, b: a + b, [jnp.exp(v - gmax) for v in gl])
    esel = []
    for j in range(MOE_EPG):
        v = jnp.zeros_like(gmax)
        for g in range(MOE_GROUPS):
            row = MOE_GROUPS + g * MOE_EPG + j
            v = jnp.where(gidx == g, logits[row:row + 1, :], v)
        esel.append(v)
    v1 = functools.reduce(jnp.maximum, esel)
    i1 = _first_index([esel[j] == v1 for j in range(MOE_EPG)], MOE_EPG)
    rest = [jnp.where(i1 == j, -jnp.inf, esel[j]) for j in range(MOE_EPG)]
    v2 = functools.reduce(jnp.maximum, rest)
    i2 = _first_index([(rest[j] == v2) & (i1 != j) for j in range(MOE_EPG)], MOE_EPG)
    e2 = jnp.exp(v2 - v1)
    w1 = gval / (1.0 + e2)
    w2 = gval * e2 / (1.0 + e2)
    a = jnp.minimum(i1, i2)
    b = jnp.maximum(i1, i2)
    wa = jnp.where(i1 < i2, w1, w2)
    wb = jnp.where(i1 < i2, w2, w1)
    cls = gidx * N_PAIRS + (a * (2 * MOE_EPG - 1 - a)) // 2 + (b - a - 1)
    cls_ref[0] = cls

    crow = lax.broadcasted_iota(jnp.int32, (CLS_ROWS, TM), 0)
    onehot = (crow == cls).astype(F32)
    prefix = jnp.dot(onehot.astype(BF16), tri_ref[...], preferred_element_type=F32)
    before = carry[:, 0:1]
    rank = jnp.sum(onehot * (prefix - 1.0 + before), axis=0, keepdims=True)
    rank_ref[0] = rank.astype(jnp.int32)
    carry[...] = carry[...] + jnp.sum(onehot, axis=1, keepdims=True)
    cnt_ref[...] = carry[...]

    wrow = lax.broadcasted_iota(jnp.int32, (128, TM), 0)
    wmat = jnp.where(wrow == 0, wa, jnp.where(wrow == 1, wb, 0.0))
    rows_ref[pl.ds(W_ROW, TM, stride=BLOCK_ROWS), :] = lax.bitcast_convert_type(wmat.T, jnp.int32)
    for k in range(W_ROW + 1, BLOCK_ROWS):
        rows_ref[pl.ds(k, TM, stride=BLOCK_ROWS), :] = jnp.zeros((TM, 128), jnp.int32)


def _router(x, g, wr, br, tri):
    n = x.shape[0]
    nt = n // TM
    return pl.pallas_call(
        _router_kernel,
        grid=(nt,),
        in_specs=[pl.BlockSpec((TM, D_MODEL), lambda i: (i, 0)), _resident((1, D_MODEL)),
                  _resident(wr.shape), _resident(br.shape), _resident(tri.shape)],
        out_specs=[pl.BlockSpec((TM * BLOCK_ROWS, 128), lambda i: (i, 0)),
                   pl.BlockSpec((1, 1, TM), lambda i: (i, 0, 0)),
                   pl.BlockSpec((1, 1, TM), lambda i: (i, 0, 0)),
                   pl.BlockSpec((CLS_ROWS, 128), lambda i: (0, 0))],
        out_shape=[jax.ShapeDtypeStruct((n * BLOCK_ROWS, 128), jnp.int32),
                   jax.ShapeDtypeStruct((nt, 1, TM), jnp.int32),
                   jax.ShapeDtypeStruct((nt, 1, TM), jnp.int32),
                   jax.ShapeDtypeStruct((CLS_ROWS, 128), F32)],
        scratch_shapes=[pltpu.VMEM((CLS_ROWS, 128), F32)],
        compiler_params=_cparams("arbitrary"),
        name="router",
    )(x, g, wr, br, tri)


def _block(ref, row):
    return ref.at[pl.ds(pl.multiple_of(row * BLOCK_ROWS, BLOCK_ROWS), BLOCK_ROWS)]


def _start_rows(copy):
    def group(g, c):
        for u in range(DMA_UNROLL):
            copy(g * DMA_UNROLL + u).start(priority=u % 2)
        return c

    lax.fori_loop(0, TM // DMA_UNROLL, group, 0)


def _wait_rows(copy):
    def wait(j, c):
        copy(j).wait()
        return c

    lax.fori_loop(0, TM, wait, 0, unroll=DMA_UNROLL)


def _scatter_kernel(tme, ztile_ref, dest_ref, rows_ref, hs_ref, zeros, sem, zsem):
    @pl.when(pl.program_id(0) == 0)
    def _():
        zeros[...] = jnp.zeros_like(zeros)

        def zcopy(c):
            start = pl.multiple_of(ztile_ref[c] * (tme * BLOCK_ROWS), BLOCK_ROWS)
            return pltpu.make_async_copy(zeros, hs_ref.at[pl.ds(start, tme * BLOCK_ROWS)], zsem)

        for c in range(2 * N_CLASSES):
            @pl.when(ztile_ref[c] >= 0)
            def _():
                zcopy(c).start()
        for c in range(2 * N_CLASSES):
            @pl.when(ztile_ref[c] >= 0)
            def _():
                zcopy(c).wait()

    def copy(j):
        return pltpu.make_async_copy(_block(rows_ref, j), _block(hs_ref, dest_ref[0, 0, j]), sem)

    _start_rows(copy)
    _wait_rows(copy)


def _scatter(zero_tiles, dest, rows, n_sorted, tme):
    nt = dest.shape[0]
    return pl.pallas_call(
        functools.partial(_scatter_kernel, tme),
        grid_spec=pltpu.PrefetchScalarGridSpec(
            num_scalar_prefetch=1,
            grid=(nt,),
            in_specs=[pl.BlockSpec((1, 1, TM), lambda i, z: (i, 0, 0), memory_space=pltpu.SMEM),
                      pl.BlockSpec((TM * BLOCK_ROWS, 128), lambda i, z: (i, 0))],
            out_specs=pl.BlockSpec(memory_space=pl.ANY),
            scratch_shapes=[pltpu.VMEM((tme * BLOCK_ROWS, 128), jnp.int32), pltpu.SemaphoreType.DMA(()),
                            pltpu.SemaphoreType.DMA(())],
        ),
        out_shape=jax.ShapeDtypeStruct((n_sorted * BLOCK_ROWS, 128), jnp.int32),
        compiler_params=_cparams("arbitrary"),
        name="row_scatter",
    )(zero_tiles, dest, rows)


def _fetch_token_blocks(dest_ref, dest_next_ref, ys_ref, buf, sem):
    i = pl.program_id(0)
    slot = lax.rem(i, 2)

    def copy(d_ref, s, j):
        return pltpu.make_async_copy(_block(ys_ref, d_ref[0, 0, j]), _block(buf.at[s], j), sem.at[s])

    @pl.when(i == 0)
    def _():
        _start_rows(functools.partial(copy, dest_ref, slot))

    @pl.when(i + 1 < pl.num_programs(0))
    def _():
        _start_rows(functools.partial(copy, dest_next_ref, 1 - slot))

    _wait_rows(functools.partial(copy, dest_ref, slot))
    return slot


def _dest_specs(nt):
    return [pl.BlockSpec((1, 1, TM), lambda i: (i, 0, 0), memory_space=pltpu.SMEM),
            pl.BlockSpec((1, 1, TM), lambda i: (jnp.minimum(i + 1, nt - 1), 0, 0), memory_space=pltpu.SMEM)]


def _gather_kernel(dest_ref, dest_next_ref, x_ref, ys_ref, o_ref, buf, sem):
    slot = _fetch_token_blocks(dest_ref, dest_next_ref, ys_ref, buf, sem)
    for k in range(D_MODEL // 128):
        lanes = slice(128 * k, 128 * (k + 1))
        o_ref[:, lanes] = x_ref[:, lanes] + buf[slot, pl.ds(k, TM, stride=BLOCK_ROWS), :]


def _gather_residual(dest, x, ys):
    n = x.shape[0]
    nt = n // TM
    return pl.pallas_call(
        _gather_kernel,
        grid=(nt,),
        in_specs=_dest_specs(nt) + [pl.BlockSpec((TM, D_MODEL), lambda i: (i, 0)),
                                    pl.BlockSpec(memory_space=pl.ANY)],
        out_specs=pl.BlockSpec((TM, D_MODEL), lambda i: (i, 0)),
        scratch_shapes=[pltpu.VMEM((2, TM * BLOCK_ROWS, 128), F32), pltpu.SemaphoreType.DMA((2,))],
        out_shape=jax.ShapeDtypeStruct((n, D_MODEL), F32),
        compiler_params=_cparams("arbitrary"),
        name="row_gather",
    )(dest, dest, x, ys)


def _expert_kernel(tme, e1_ref, e2_ref, src_ref, hs_ref, wgu1_ref, wgu2_ref, wd1_ref, wd2_ref, ys_ref):
    i = pl.program_id(0)

    @pl.when(src_ref[i] == i)
    def _():
        words = [hs_ref[pl.ds(k, tme, stride=BLOCK_ROWS), :] for k in range(D_MODEL // 256)]
        lo = [lax.bitcast_convert_type(lax.shift_left(w, 16), F32).astype(BF16) for w in words]
        hi = [lax.bitcast_convert_type(w & HIGH_HALF, F32).astype(BF16) for w in words]
        h = jnp.concatenate(lo + hi, axis=1)
        wts = lax.bitcast_convert_type(hs_ref[pl.ds(W_ROW, tme, stride=BLOCK_ROWS), :], F32)

        def expert(wgu_ref, wd_ref):
            gu = jnp.dot(h, wgu_ref[0], preferred_element_type=F32)
            gate = gu[:, 0:D_EXPERT]
            act = gate * jax.nn.sigmoid(gate) * gu[:, D_EXPERT:]
            return jnp.dot(act.astype(BF16), wd_ref[0], preferred_element_type=F32)

        y = wts[:, 0:1] * expert(wgu1_ref, wd1_ref) + wts[:, 1:2] * expert(wgu2_ref, wd2_ref)
        for k in range(D_MODEL // 128):
            ys_ref[pl.ds(k, tme, stride=BLOCK_ROWS), :] = y[:, 128 * k:128 * (k + 1)]

    @pl.when(src_ref[i] != i)
    def _():
        ys_ref[...] = jnp.zeros_like(ys_ref)


def _experts(tile_e1, tile_e2, tile_src, hs, wgu, wd, tme):
    n_sorted = hs.shape[0] // BLOCK_ROWS
    nt = n_sorted // tme
    return pl.pallas_call(
        functools.partial(_expert_kernel, tme),
        grid_spec=pltpu.PrefetchScalarGridSpec(
            num_scalar_prefetch=3,
            grid=(nt,),
            in_specs=[
                pl.BlockSpec((tme * BLOCK_ROWS, 128), lambda i, e1, e2, src: (src[i], 0)),
                pl.BlockSpec((1, D_MODEL, 2 * D_EXPERT), lambda i, e1, e2, src: (e1[i], 0, 0)),
                pl.BlockSpec((1, D_MODEL, 2 * D_EXPERT), lambda i, e1, e2, src: (e2[i], 0, 0)),
                pl.BlockSpec((1, D_EXPERT, D_MODEL), lambda i, e1, e2, src: (e1[i], 0, 0)),
                pl.BlockSpec((1, D_EXPERT, D_MODEL), lambda i, e1, e2, src: (e2[i], 0, 0)),
            ],
            out_specs=pl.BlockSpec((tme * BLOCK_ROWS, 128), lambda i, e1, e2, src: (i, 0)),
        ),
        out_shape=jax.ShapeDtypeStruct((n_sorted * BLOCK_ROWS, 128), F32),
        compiler_params=_cparams("arbitrary"),
        name="experts",
    )(tile_e1, tile_e2, tile_src, hs, wgu, wgu, wd, wd)


_PAIR_A = np.array([a for a in range(MOE_EPG) for b in range(a + 1, MOE_EPG)], np.int32)
_PAIR_B = np.array([b for a in range(MOE_EPG) for b in range(a + 1, MOE_EPG)], np.int32)


def _moe(x, lp):
    n = x.shape[0]
    rows, cls, rank, counts = _router(x, lp["norm_ffn_g"], lp["wr"], lp["br"], lp["tri"])
    tme = TME_BIG if n >= 4 * N_CLASSES * TME_BIG else TME
    n_tiles = n // tme + N_CLASSES
    cnt = counts[:N_CLASSES, 0].astype(jnp.int32)
    tiles_per_class = (cnt + tme - 1) // tme
    tile_end = jnp.cumsum(tiles_per_class)
    seg_start = (tile_end - tiles_per_class) * tme
    last_tile = jnp.where(tiles_per_class > 0, tile_end - 1, -1)
    tile_id = jnp.arange(n_tiles, dtype=jnp.int32)
    tile_cls = jnp.sum((tile_id[:, None] >= tile_end[None, :]).astype(jnp.int32), axis=1)
    tile_src = jnp.minimum(tile_id, tile_end[-1] - 1)
    tile_cls = jnp.minimum(tile_cls, N_CLASSES - 1)
    class_id = jnp.arange(N_CLASSES, dtype=jnp.int32)
    class_np = np.arange(N_CLASSES)
    class_e1 = jnp.asarray((class_np // N_PAIRS) * MOE_EPG + _PAIR_A[class_np % N_PAIRS], jnp.int32)
    class_e2 = jnp.asarray((class_np // N_PAIRS) * MOE_EPG + _PAIR_B[class_np % N_PAIRS], jnp.int32)
    tile_hot = (tile_cls[:, None] == class_id[None, :]).astype(jnp.int32)
    tile_e1 = jnp.sum(tile_hot * class_e1[None, :], axis=1)
    tile_e2 = jnp.sum(tile_hot * class_e2[None, :], axis=1)
    dest = rank + jnp.sum(jnp.where(cls[..., None] == class_id, seg_start, 0), axis=-1)
    spare = tile_end[-1] + jnp.arange(N_CLASSES, dtype=jnp.int32)
    zero_tiles = jnp.concatenate([last_tile, jnp.where(spare < n_tiles, spare, -1)]).astype(jnp.int32)
    hs = _scatter(zero_tiles, dest, rows, n_tiles * tme, tme)
    return dest, _experts(tile_e1, tile_e2, tile_src, hs, lp["wgu"], lp["wd"], tme)


def _prep_layer(l, norm_mix_g, w_in, b_in, q_norm_g, k_norm_g, na_rpb, sc_conv_w, cf_conv_w, cf_conv_b,
                cf_norm_g, cf_norm_b, w_branch, w_out, norm_ffn_g, router_group_w, router_group_b,
                router_expert_w, router_expert_b, expert_w_gate, expert_w_up, expert_w_down):
    head = np.arange(NA_WIDTH) // NA_HEAD_DIM
    bd = jnp.asarray((head[:, None] == head[None, :]).astype(np.float32) / NA_HEAD_DIM, BF16)
    wr = jnp.zeros((CLS_ROWS, D_MODEL), F32)
    wr = wr.at[0:MOE_GROUPS].set(router_group_w[l].T.astype(F32))
    wr = wr.at[MOE_GROUPS:MOE_GROUPS + MOE_EXPERTS].set(router_expert_w[l].T.astype(F32))
    br = jnp.zeros((CLS_ROWS,), F32)
    br = br.at[0:MOE_GROUPS].set(router_group_b[l].astype(F32))
    br = br.at[MOE_GROUPS:MOE_GROUPS + MOE_EXPERTS].set(router_expert_b[l].astype(F32))
    tri = jnp.asarray(np.triu(np.ones((TM, TM), np.float32)), BF16)
    return dict(
        norm_mix_g=norm_mix_g[l].reshape(1, D_MODEL).astype(F32),
        w_in=w_in[l].astype(BF16),
        b_in=b_in[l].reshape(1, IN_COLS).astype(F32),
        qg=(jnp.tile(q_norm_g[l].astype(F32), NA_HEADS) * (NA_HEAD_DIM ** -0.5 * LOG2E)).reshape(1, NA_WIDTH),
        kg=jnp.tile(k_norm_g[l].astype(F32), NA_HEADS).reshape(1, NA_WIDTH),
        bd=bd,
        tbias=_bias_table(na_rpb[l]),
        scw=sc_conv_w[l].astype(F32),
        cfw=cf_conv_w[l].astype(F32),
        cfb=cf_conv_b[l].reshape(1, CF_WIDTH).astype(F32),
        cfg=cf_norm_g[l].reshape(1, CF_WIDTH).astype(F32),
        cfbeta=cf_norm_b[l].reshape(1, CF_WIDTH).astype(F32),
        wb=w_branch[l].astype(BF16),
        wo=w_out[l].astype(BF16),
        norm_ffn_g=norm_ffn_g[l].reshape(1, D_MODEL).astype(F32),
        wr=jnp.concatenate([wr.astype(BF16), (wr - wr.astype(BF16).astype(F32)).astype(BF16)], axis=0),
        br=jnp.broadcast_to(br[:, None], (CLS_ROWS, 128)),
        tri=tri,
        wgu=jnp.concatenate([expert_w_gate[l], expert_w_up[l]], axis=-1).astype(BF16),
        wd=expert_w_down[l].astype(BF16),
    )


def _trunk(x, layers):
    bn, seq_len, _ = x.shape
    assert seq_len % TM == 0 and seq_len % (RB * GRID_W) == 0
    xf = x.reshape(bn * seq_len, D_MODEL)
    pending = None
    for lp in layers:
        if pending is None:
            qkv, scu, gates = _inproj(xf, lp)
        else:
            xf, qkv, scu, gates = _inproj(xf, lp, pending)
        ya = _attention(qkv, bn, seq_len, lp["tbias"])
        xf = _merge(xf, ya, scu, gates, seq_len, lp["scw"], lp["cfw"], lp["cfb"], lp["cfg"], lp["cfbeta"],
                    lp["wb"], lp["wo"])
        pending = _moe(xf, lp)
    return _gather_residual(pending[0], xf, pending[1]).reshape(bn, seq_len, D_MODEL)


def kernel(x_prompt, x_sample, norm_mix_g, w_in, b_in, q_norm_g, k_norm_g, na_rpb, sc_conv_w, cf_conv_w, cf_conv_b, cf_norm_g, cf_norm_b, w_branch, w_out, norm_ffn_g, router_group_w, router_group_b, router_expert_w, router_expert_b, expert_w_gate, expert_w_up, expert_w_down):
    params = (norm_mix_g, w_in, b_in, q_norm_g, k_norm_g, na_rpb, sc_conv_w, cf_conv_w, cf_conv_b, cf_norm_g,
              cf_norm_b, w_branch, w_out, norm_ffn_g, router_group_w, router_group_b, router_expert_w,
              router_expert_b, expert_w_gate, expert_w_up, expert_w_down)
    layers = [_prep_layer(l, *params) for l in range(norm_mix_g.shape[0])]
    return (_trunk(x_prompt, layers), _trunk(x_sample, layers))
```
